```python
import jax, jax.numpy as jnp
from jax import lax
import numpy as np

D_MODEL = 2048
BATCH = 2
SEQ = 8192
DEPTH = 1

HEAD_DIM = 128
N_ATTN_HEADS = 8
ATTN_WIDTH = N_ATTN_HEADS * HEAD_DIM
CONV_CHANNELS = D_MODEL - ATTN_WIDTH
CONV_GROUPS = 8
MIX_WIDTH = ATTN_WIDTH + CONV_CHANNELS
IN_COLS = 3 * ATTN_WIDTH + 2 * CONV_CHANNELS
MOBA_BLOCK = 256
MOBA_TOPK = 3
Q_CHUNK = 64
CONV_WIDTH = 31
D_FF = 4 * D_MODEL
RMS_EPS = 1e-6
LN_EPS = 1e-5

kernel_name = 'hybrid_moba_conformer_layer'


def rms_norm(x, g):
    xf = x.astype(jnp.float32)
    y = xf * lax.rsqrt(jnp.mean(xf * xf, axis=-1, keepdims=True) + RMS_EPS)
    return (y * g.astype(jnp.float32)).astype(x.dtype)


def layer_norm(x, g, b):
    xf = x.astype(jnp.float32)
    mu = jnp.mean(xf, axis=-1, keepdims=True)
    var = jnp.mean(jnp.square(xf - mu), axis=-1, keepdims=True)
    y = (xf - mu) * lax.rsqrt(var + LN_EPS)
    return (y * g.astype(jnp.float32) + b.astype(jnp.float32)).astype(x.dtype)


def alibi_slopes(n_heads):
    return jnp.asarray(2.0 ** (-8.0 * np.arange(1, n_heads + 1) / n_heads), dtype=jnp.float32)


def moba_attention(q, k, v):
    B, H, S, Dh = q.shape
    L = MOBA_BLOCK
    n_blocks = -(-S // L)
    pad = n_blocks * L - S
    kp = jnp.pad(k, ((0, 0), (0, 0), (0, pad), (0, 0)))
    vp = jnp.pad(v, ((0, 0), (0, 0), (0, pad), (0, 0)))
    k_blk = kp.reshape(B, H, n_blocks, L, Dh)
    v_blk = vp.reshape(B, H, n_blocks, L, Dh)
    k_mean = jnp.mean(k_blk.astype(jnp.float32), axis=3)
    topk = min(MOBA_TOPK, n_blocks)
    scale = Dh ** -0.5
    slopes = alibi_slopes(H)
    n_chunks = S // Q_CHUNK
    q_chunks = q.reshape(B, H, n_chunks, Q_CHUNK, Dh).transpose(2, 0, 1, 3, 4)
    gather = jax.vmap(jax.vmap(lambda blocks, idx: blocks[idx]))
    neg_inf = -jnp.inf

    def chunk_fn(args):
        c, qc = args
        q0 = c * Q_CHUNK
        blk = q0 // L
        q_pos = q0 + jnp.arange(Q_CHUNK)
        gate = jnp.einsum('bhqd,bhnd->bhqn', qc.astype(jnp.float32), k_mean)
        gate = jnp.where(jnp.arange(n_blocks) < blk, gate, neg_inf)
        _, sel = lax.top_k(gate, topk)
        sel_valid = jnp.arange(topk) < blk
        k_sel = gather(k_blk, sel)
        v_sel = gather(v_blk, sel)
        s_sel = jnp.einsum('bhqd,bhqrld->bhqrl', qc, k_sel,
                           preferred_element_type=jnp.float32) * scale
        k_pos_sel = sel[..., None] * L + jnp.arange(L)
        dist_sel = jnp.abs(q_pos[None, None, :, None, None] - k_pos_sel).astype(jnp.float32)
        s_sel = jnp.where(sel_valid[:, None],
                          s_sel - slopes[None, :, None, None, None] * dist_sel, neg_inf)
        k_own = lax.dynamic_slice_in_dim(kp, blk * L, L, axis=2)
        v_own = lax.dynamic_slice_in_dim(vp, blk * L, L, axis=2)
        s_own = jnp.einsum('bhqd,bhld->bhql', qc, k_own,
                           preferred_element_type=jnp.float32) * scale
        k_pos_own = blk * L + jnp.arange(L)
        dist_own = jnp.abs(q_pos[:, None] - k_pos_own[None, :]).astype(jnp.float32)
        causal = k_pos_own[None, :] <= q_pos[:, None]
        s_own = jnp.where(causal, s_own - slopes[:, None, None] * dist_own, neg_inf)
        s = jnp.concatenate([s_sel.reshape(B, H, Q_CHUNK, topk * L), s_own], axis=-1)
        p = jax.nn.softmax(s, axis=-1)
        p_sel = p[..., :topk * L].reshape(B, H, Q_CHUNK, topk, L).astype(v.dtype)
        p_own = p[..., topk * L:].astype(v.dtype)
        out = (jnp.einsum('bhqrl,bhqrld->bhqd', p_sel, v_sel, preferred_element_type=jnp.float32)
               + jnp.einsum('bhql,bhld->bhqd', p_own, v_own, preferred_element_type=jnp.float32))
        return out.astype(qc.dtype)

    out = lax.map(chunk_fn, (jnp.arange(n_chunks), q_chunks))
    return out.transpose(1, 0, 3, 2, 4).reshape(B, S, H * Dh)


def conformer_conv(u, b_glu, w_dw, b_dw, ln_g, ln_b):
    u = u + b_glu
    val, gt = jnp.split(u, 2, axis=-1)
    h = val * jax.nn.sigmoid(gt)
    C = h.shape[-1]
    rhs = w_dw.reshape(CONV_WIDTH, 1, C)
    y = lax.conv_general_dilated(h, rhs, window_strides=(1,),
                                 padding=[(CONV_WIDTH - 1, 0)],
                                 dimension_numbers=('NWC', 'WIO', 'NWC'),
                                 feature_group_count=C)
    y = layer_norm(y + b_dw, ln_g, ln_b)
    return jax.nn.silu(y)


def setup_inputs(seed: int = 0) -> dict:
    key = jax.random.key(seed)
    ks = jax.random.split(key, 14)
    f32 = jnp.float32

    def nrm(k, shape, scale):
        return jax.random.normal(k, shape, f32) * scale

    def gain(k, shape):
        return 1.0 + 0.02 * jax.random.normal(k, shape, f32)

    return {
        'x': nrm(ks[0], (BATCH, SEQ, D_MODEL), 1.0),
        'g_mix_pre': gain(ks[1], (DEPTH, D_MODEL)),
        'w_in': nrm(ks[2], (DEPTH, D_MODEL, IN_COLS), D_MODEL ** -0.5),
        'b_glu': nrm(ks[3], (DEPTH, 2 * CONV_CHANNELS), 0.02),
        'w_dw': nrm(ks[4], (DEPTH, CONV_WIDTH, CONV_CHANNELS), CONV_WIDTH ** -0.5),
        'b_dw': nrm(ks[5], (DEPTH, CONV_CHANNELS), 0.02),
        'ln_conv_g': gain(ks[6], (DEPTH, CONV_CHANNELS)),
        'ln_conv_b': nrm(ks[7], (DEPTH, CONV_CHANNELS), 0.02),
        'w_out': nrm(ks[8], (DEPTH, MIX_WIDTH, D_MODEL), MIX_WIDTH ** -0.5),
        'g_mix_post': gain(ks[9], (DEPTH, D_MODEL)),
        'g_ffn_pre': gain(ks[10], (DEPTH, D_MODEL)),
        'w_ff1': nrm(ks[11], (DEPTH, D_MODEL, D_FF), D_MODEL ** -0.5),
        'w_ff2': nrm(ks[12], (DEPTH, D_FF, D_MODEL), D_FF ** -0.5),
        'g_ffn_post': gain(ks[13], (DEPTH, D_MODEL)),
    }


def reference(x, g_mix_pre, w_in, b_glu, w_dw, b_dw, ln_conv_g, ln_conv_b, w_out,
              g_mix_post, g_ffn_pre, w_ff1, w_ff2, g_ffn_post):
    B, S, _ = x.shape
    h = x
    for l in range(DEPTH):
        a = rms_norm(h, g_mix_pre[l])
        proj = a @ w_in[l]
        q = proj[..., :ATTN_WIDTH]
        k = proj[..., ATTN_WIDTH:2 * ATTN_WIDTH]
        v = proj[..., 2 * ATTN_WIDTH:3 * ATTN_WIDTH]
        u = proj[..., 3 * ATTN_WIDTH:]
        to_heads = lambda t: t.reshape(B, S, N_ATTN_HEADS, HEAD_DIM).transpose(0, 2, 1, 3)
        attn = moba_attention(to_heads(q), to_heads(k), to_heads(v))
        conv = conformer_conv(u, b_glu[l], w_dw[l], b_dw[l], ln_conv_g[l], ln_conv_b[l])
        mixed = jnp.concatenate([attn, conv], axis=-1) @ w_out[l]
        h = h + rms_norm(mixed, g_mix_post[l])
        f = rms_norm(h, g_ffn_pre[l])
        f = jnp.square(jax.nn.relu(f @ w_ff1[l])) @ w_ff2[l]
        h = h + rms_norm(f, g_ffn_post[l])
    return h
```

```python
import functools

import numpy as np
import jax
import jax.numpy as jnp
from jax import lax
from jax.experimental import pallas as pl
from jax.experimental.pallas import tpu as pltpu

F32 = jnp.float32
BF16 = jnp.bfloat16

HEAD_DIM = 128
N_HEADS = 8
ATTN_WIDTH = N_HEADS * HEAD_DIM
MOBA_BLOCK = 256
MOBA_TOPK = 3
CONV_WIDTH = 31
RMS_EPS = 1e-6
LN_EPS = 1e-5

V7X_VMEM_LIMIT_BYTES = 60000 * 1024
SUBLANES = 8

IN_PROJ_ROWS = 512
IN_PROJ_COLS = 512
MIX_ROWS = 256
MIX_COLS = 512
CONV_HALO = 32
CONV_ROW_CHUNK = 64
CONV_LANE_CHUNK = 256
FFN_ROWS = 512
FFN_COLS = 1024

MASKED = -1e30
NT_DIMS = (((1,), (1,)), ((), ()))


def _resident(shape):
    return pl.BlockSpec(shape, lambda *_: (0,) * len(shape), pipeline_mode=pl.Buffered(1))


def _alibi_slopes(n_heads):
    return jnp.asarray(2.0 ** (-8.0 * np.arange(1, n_heads + 1) / n_heads), dtype=F32)


def _in_proj_kernel(x_ref, g_ref, wqk_ref, wvt_ref, wglu_ref, bglu_ref,
                    qk_ref, vt_ref, kmean_ref, h_ref, *, scale):
    rows = x_ref.shape[1]
    conv_ch = h_ref.shape[2]
    x = x_ref[0]
    ms = jnp.mean(x * x, axis=-1, keepdims=True)
    a = (x * lax.rsqrt(ms + RMS_EPS) * g_ref[...]).astype(BF16)

    for c in range(0, 2 * ATTN_WIDTH, IN_PROJ_COLS):
        r = jnp.dot(a, wqk_ref[:, c:c + IN_PROJ_COLS], preferred_element_type=F32)
        if c < ATTN_WIDTH:
            r = r * scale
        else:
            ck = c - ATTN_WIDTH
            blocks = r.reshape(rows // MOBA_BLOCK, MOBA_BLOCK, IN_PROJ_COLS)
            kmean_ref[0, 0, :, ck:ck + IN_PROJ_COLS] = jnp.mean(blocks, axis=1)
        qk_ref[0, :, c:c + IN_PROJ_COLS] = r.astype(BF16)

    for rc in range(0, ATTN_WIDTH, MOBA_BLOCK):
        r = lax.dot_general(wvt_ref[rc:rc + MOBA_BLOCK, :], a, NT_DIMS,
                            preferred_element_type=F32).astype(BF16)
        for blk in range(rows // MOBA_BLOCK):
            vt_ref[0, blk, rc:rc + MOBA_BLOCK, :] = r[:, blk * MOBA_BLOCK:(blk + 1) * MOBA_BLOCK]

    for c in range(0, conv_ch, IN_PROJ_COLS):
        val = jnp.dot(a, wglu_ref[:, c:c + IN_PROJ_COLS], preferred_element_type=F32)
        val = val + bglu_ref[:, c:c + IN_PROJ_COLS]
        gt = jnp.dot(a, wglu_ref[:, conv_ch + c:conv_ch + c + IN_PROJ_COLS], preferred_element_type=F32)
        gt = gt + bglu_ref[:, conv_ch + c:conv_ch + c + IN_PROJ_COLS]
        h_ref[0, :, c:c + IN_PROJ_COLS] = val * (1.0 / (1.0 + jnp.exp(-gt)))


def _in_proj(x, g, w_qk, w_vt, w_glu, b_glu):
    B, S, D = x.shape
    conv_ch = w_glu.shape[1] // 2
    rows = IN_PROJ_ROWS
    n_blk = rows // MOBA_BLOCK
    grid = (B, S // rows)
    out_shape = (
        jax.ShapeDtypeStruct((B, S, 2 * ATTN_WIDTH), BF16),
        jax.ShapeDtypeStruct((B, S // MOBA_BLOCK, ATTN_WIDTH, MOBA_BLOCK), BF16),
        jax.ShapeDtypeStruct((B, S // rows, n_blk, ATTN_WIDTH), F32),
        jax.ShapeDtypeStruct((B, S, conv_ch), F32),
    )
    return pl.pallas_call(
        functools.partial(_in_proj_kernel, scale=HEAD_DIM ** -0.5),
        grid=grid,
        in_specs=[
            pl.BlockSpec((1, rows, D), lambda b, i: (b, i, 0)),
            _resident((1, D)),
            _resident(w_qk.shape),
            _resident(w_vt.shape),
            _resident(w_glu.shape),
            _resident((1, 2 * conv_ch)),
        ],
        out_specs=(
            pl.BlockSpec((1, rows, 2 * ATTN_WIDTH), lambda b, i: (b, i, 0)),
            pl.BlockSpec((1, n_blk, ATTN_WIDTH, MOBA_BLOCK), lambda b, i: (b, i, 0, 0)),
            pl.BlockSpec((1, 1, n_blk, ATTN_WIDTH), lambda b, i: (b, i, 0, 0)),
            pl.BlockSpec((1, rows, conv_ch), lambda b, i: (b, i, 0)),
        ),
        out_shape=out_shape,
        compiler_params=pltpu.CompilerParams(
            dimension_semantics=("arbitrary", "arbitrary"),
            vmem_limit_bytes=V7X_VMEM_LIMIT_BYTES),
        name="in_proj",
    )(x, g, w_qk, w_vt, w_glu, b_glu)


def _moba_kernel(slopes_ref, q_ref, k_ref, vt_ref, kmean_ref, o_ref, pen_ref, colb_ref):
    L = MOBA_BLOCK
    n_blocks = vt_ref.shape[1]
    slope = slopes_ref[pl.program_id(1)]

    key_local = lax.broadcasted_iota(jnp.int32, (L, L), 0)
    qry_local = lax.broadcasted_iota(jnp.int32, (L, L), 1)
    colb_ref[...] = slope * key_local.astype(F32)
    kmean = kmean_ref[0].astype(BF16)
    blk_id = lax.broadcasted_iota(jnp.int32, (n_blocks, L), 0).astype(F32)

    def q_tile(qi, carry):
        q0 = pl.multiple_of(qi * L, L)
        q = q_ref[0, pl.ds(q0, L), :]

        gate = lax.dot_general(kmean, q, NT_DIMS, preferred_element_type=F32)
        gate = jnp.where(blk_id < qi.astype(F32), gate, -jnp.inf)
        sel = jnp.zeros((n_blocks, L), F32)
        for r in range(MOBA_TOPK):
            best = jnp.max(gate, axis=0, keepdims=True)
            first = jnp.min(jnp.where(gate == best, blk_id, float(n_blocks)), axis=0, keepdims=True)
            hit = blk_id == first
            sel = jnp.maximum(sel, jnp.where(hit, (qi > r).astype(F32), 0.0))
            gate = jnp.where(hit, -jnp.inf, gate)
        pen_ref[...] = jnp.where(sel > 0.0, 0.0, MASKED)

        k_own = k_ref[0, pl.ds(q0, L), :]
        s = lax.dot_general(k_own, q, NT_DIMS, preferred_element_type=F32)
        s = jnp.where(key_local <= qry_local, s + colb_ref[...], -jnp.inf)
        m = jnp.max(s, axis=0, keepdims=True)
        p = jnp.exp(s - m)
        l = jnp.sum(p, axis=0, keepdims=True)
        acc = jnp.dot(vt_ref[0, qi], p.astype(BF16), preferred_element_type=F32)

        def past_block(j, state):
            m, l, acc = state
            k0 = pl.multiple_of(j * L, L)
            s = lax.dot_general(k_ref[0, pl.ds(k0, L), :], q, NT_DIMS, preferred_element_type=F32)
            s = s + colb_ref[...]
            row = pen_ref[pl.ds(j, 1), :] - slope * float(L) * (qi - j).astype(F32)
            m_new = jnp.maximum(m, jnp.max(s, axis=0, keepdims=True) + row)
            alpha = jnp.exp(m - m_new)
            p = jnp.exp(s + (row - m_new))
            l = alpha * l + jnp.sum(p, axis=0, keepdims=True)
            acc = alpha * acc + jnp.dot(vt_ref[0, j], p.astype(BF16), preferred_element_type=F32)
            return m_new, l, acc

        m, l, acc = lax.fori_loop(0, qi, past_block, (m, l, acc))
        o_ref[0, pl.ds(q0, L), :] = (acc / l).T.astype(o_ref.dtype)
        return carry

    lax.fori_loop(0, n_blocks, q_tile, 0)


def _moba(qk, vt, kmean):
    B, S, _ = qk.shape
    n_blocks = S // MOBA_BLOCK
    return pl.pallas_call(
        _moba_kernel,
        grid=(B, N_HEADS),
        in_specs=[
            pl.BlockSpec(memory_space=pltpu.SMEM),
            pl.BlockSpec((1, S, HEAD_DIM), lambda b, h: (b, 0, h)),
            pl.BlockSpec((1, S, HEAD_DIM), lambda b, h: (b, 0, N_HEADS + h)),
            pl.BlockSpec((1, n_blocks, HEAD_DIM, MOBA_BLOCK), lambda b, h: (b, 0, h, 0)),
            pl.BlockSpec((1, n_blocks, HEAD_DIM), lambda b, h: (b, 0, h)),
        ],
        out_specs=pl.BlockSpec((1, S, HEAD_DIM), lambda b, h: (b, 0, h)),
        out_shape=jax.ShapeDtypeStruct((B, S, ATTN_WIDTH), BF16),
        scratch_shapes=[
            pltpu.VMEM((n_blocks, MOBA_BLOCK), F32),
            pltpu.VMEM((MOBA_BLOCK, MOBA_BLOCK), F32),
        ],
        compiler_params=pltpu.CompilerParams(
            dimension_semantics=("arbitrary", "arbitrary"),
            vmem_limit_bytes=V7X_VMEM_LIMIT_BYTES),
        name="moba",
    )(_alibi_slopes(N_HEADS), qk, qk, vt, kmean)


def _mix_kernel(x_ref, attn_ref, hmain_ref, hhalo_ref, wdw_ref, bdw_ref, lng_ref, lnb_ref,
                wo_ref, gpost_ref, gpre_ref, h1_ref, f_ref, ext_ref, conv_ref):
    rows = x_ref.shape[1]
    conv_ch = hmain_ref.shape[2]
    first_tile = pl.program_id(1) == 0

    ext_ref[0:CONV_HALO, :] = jnp.where(first_tile, 0.0, hhalo_ref[0])
    ext_ref[CONV_HALO:CONV_HALO + rows, :] = hmain_ref[0]

    shift0 = CONV_HALO - (CONV_WIDTH - 1)
    for r0 in range(0, rows, CONV_ROW_CHUNK):
        parts = []
        for c0 in range(0, conv_ch, CONV_LANE_CHUNK):
            acc = jnp.zeros((CONV_ROW_CHUNK, CONV_LANE_CHUNK), F32)
            for w in range(CONV_WIDTH):
                tap = ext_ref[r0 + shift0 + w:r0 + shift0 + w + CONV_ROW_CHUNK, c0:c0 + CONV_LANE_CHUNK]
                acc = acc + tap * wdw_ref[w:w + 1, c0:c0 + CONV_LANE_CHUNK]
            parts.append(acc + bdw_ref[:, c0:c0 + CONV_LANE_CHUNK])
        y = jnp.concatenate(parts, axis=-1)
        mu = jnp.mean(y, axis=-1, keepdims=True)
        var = jnp.mean(jnp.square(y - mu), axis=-1, keepdims=True)
        y = (y - mu) * lax.rsqrt(var + LN_EPS) * lng_ref[...] + lnb_ref[...]
        y = y * (1.0 / (1.0 + jnp.exp(-y)))
        conv_ref[r0:r0 + CONV_ROW_CHUNK, :] = y.astype(BF16)

    attn = attn_ref[0]
    conv = conv_ref[...]
    attn_w = attn.shape[1]
    cols = []
    for c in range(0, wo_ref.shape[1], MIX_COLS):
        r = jnp.dot(attn, wo_ref[0:attn_w, c:c + MIX_COLS], preferred_element_type=F32)
        r = r + jnp.dot(conv, wo_ref[attn_w:attn_w + conv_ch, c:c + MIX_COLS], preferred_element_type=F32)
        cols.append(r)
    mixed = jnp.concatenate(cols, axis=-1)
    ms = jnp.mean(mixed * mixed, axis=-1, keepdims=True)
    h1 = x_ref[0] + mixed * lax.rsqrt(ms + RMS_EPS) * gpost_ref[...]
    h1_ref[0] = h1
    ms1 = jnp.mean(h1 * h1, axis=-1, keepdims=True)
    f_ref[0] = (h1 * lax.rsqrt(ms1 + RMS_EPS) * gpre_ref[...]).astype(BF16)


def _mix(x, attn, hglu, w_dw, b_dw, ln_g, ln_b, w_out, g_post, g_pre):
    B, S, D = x.shape
    conv_ch = hglu.shape[2]
    rows = MIX_ROWS
    halo_per_tile = rows // CONV_HALO
    return pl.pallas_call(
        _mix_kernel,
        grid=(B, S // rows),
        in_specs=[
            pl.BlockSpec((1, rows, D), lambda b, i: (b, i, 0)),
            pl.BlockSpec((1, rows, attn.shape[2]), lambda b, i: (b, i, 0)),
            pl.BlockSpec((1, rows, conv_ch), lambda b, i: (b, i, 0)),
            pl.BlockSpec((1, CONV_HALO, conv_ch),
                         lambda b, i: (b, jnp.maximum(i * halo_per_tile - 1, 0), 0)),
            _resident(w_dw.shape),
            _resident((1, conv_ch)),
            _resident((1, conv_ch)),
            _resident((1, conv_ch)),
            _resident(w_out.shape),
            _resident((1, D)),
            _resident((1, D)),
        ],
        out_specs=(
            pl.BlockSpec((1, rows, D), lambda b, i: (b, i, 0)),
            pl.BlockSpec((1, rows, D), lambda b, i: (b, i, 0)),
        ),
        out_shape=(jax.ShapeDtypeStruct((B, S, D), F32), jax.ShapeDtypeStruct((B, S, D), BF16)),
        scratch_shapes=[
            pltpu.VMEM((CONV_HALO + rows, conv_ch), F32),
            pltpu.VMEM((rows, conv_ch), BF16),
        ],
        compiler_params=pltpu.CompilerParams(
            dimension_semantics=("arbitrary", "arbitrary"),
            vmem_limit_bytes=V7X_VMEM_LIMIT_BYTES),
        name="mix",
    )(x, attn, hglu, hglu, w_dw, b_dw, ln_g, ln_b, w_out, g_post, g_pre)


def _ffn_kernel(f_ref, h1_ref, w1_ref, w2_ref, g_ref, o_ref, acc_ref):
    kf = pl.program_id(2)

    @pl.when(kf == 0)
    def _():
        acc_ref[...] = jnp.zeros_like(acc_ref)

    u = jnp.dot(f_ref[0], w1_ref[...], preferred_element_type=F32)
    u = jnp.square(jnp.maximum(u, 0.0)).astype(BF16)
    acc_ref[...] += jnp.dot(u, w2_ref[...], preferred_element_type=F32)

    @pl.when(kf == pl.num_programs(2) - 1)
    def _():
        a = acc_ref[...]
        ms = jnp.mean(a * a, axis=-1, keepdims=True)
        o_ref[0] = h1_ref[0] + a * lax.rsqrt(ms + RMS_EPS) * g_ref[...]


def _ffn(f, h1, w1, w2, g):
    B, S, D = h1.shape
    d_ff = w1.shape[1]
    rows, cols = FFN_ROWS, FFN_COLS
    return pl.pallas_call(
        _ffn_kernel,
        grid=(B, S // rows, d_ff // cols),
        in_specs=[
            pl.BlockSpec((1, rows, D), lambda b, i, k: (b, i, 0)),
            pl.BlockSpec((1, rows, D), lambda b, i, k: (b, i, 0)),
            pl.BlockSpec((D, cols), lambda b, i, k: (0, k)),
            pl.BlockSpec((cols, D), lambda b, i, k: (k, 0)),
            _resident((1, D)),
        ],
        out_specs=pl.BlockSpec((1, rows, D), lambda b, i, k: (b, i, 0)),
        out_shape=jax.ShapeDtypeStruct((B, S, D), F32),
        scratch_shapes=[pltpu.VMEM((rows, D), F32)],
        compiler_params=pltpu.CompilerParams(
            dimension_semantics=("arbitrary", "arbitrary", "arbitrary"),
            vmem_limit_bytes=V7X_VMEM_LIMIT_BYTES),
        name="ffn",
    )(f, h1, w1, w2, g)


def kernel(x, g_mix_pre, w_in, b_glu, w_dw, b_dw, ln_conv_g, ln_conv_b, w_out, g_mix_post,
           g_ffn_pre, w_ff1, w_ff2, g_ffn_post):
    B, S, D = x.shape
    depth = w_in.shape[0]
    assert S % IN_PROJ_ROWS == 0 and S % MIX_ROWS == 0 and S % FFN_ROWS == 0
    assert IN_PROJ_ROWS % MOBA_BLOCK == 0 and MIX_ROWS % CONV_HALO == 0
    assert CONV_HALO >= CONV_WIDTH - 1 and CONV_HALO % SUBLANES == 0
    row = lambda v: v.reshape(1, -1)

    h = x
    for l in range(depth):
        w_qk = w_in[l, :, :2 * ATTN_WIDTH].astype(BF16)
        w_vt = w_in[l, :, 2 * ATTN_WIDTH:3 * ATTN_WIDTH].T.astype(BF16)
        w_glu = w_in[l, :, 3 * ATTN_WIDTH:].astype(BF16)

        qk, vt, kmean, hglu = _in_proj(h, row(g_mix_pre[l]), w_qk, w_vt, w_glu, row(b_glu[l]))
        kmean = kmean.reshape(B, S // MOBA_BLOCK, ATTN_WIDTH)
        attn = _moba(qk, vt, kmean)
        h1, f = _mix(h, attn, hglu, w_dw[l], row(b_dw[l]), row(ln_conv_g[l]), row(ln_conv_b[l]),
                     w_out[l].astype(BF16), row(g_mix_post[l]), row(g_ffn_pre[l]))
        h = _ffn(f, h1, w_ff1[l].astype(BF16), w_ff2[l].astype(BF16), row(g_ffn_post[l]))
    return h
```

```python
import functools

import numpy as np
import jax
import jax.numpy as jnp
from jax import lax
from jax.experimental import pallas as pl
from jax.experimental.pallas import tpu as pltpu

F32 = jnp.float32
BF16 = jnp.bfloat16

HEAD_DIM = 128
N_HEADS = 8
ATTN_WIDTH = N_HEADS * HEAD_DIM
MOBA_BLOCK = 256
MOBA_TOPK = 3
CONV_WIDTH = 31
RMS_EPS = 1e-6
LN_EPS = 1e-5

V7X_VMEM_LIMIT_BYTES = 60000 * 1024
SUBLANES = 8

IN_PROJ_ROWS = 512
IN_PROJ_COLS = 512
MIX_ROWS = 256
MIX_COLS = 512
CONV_HALO = 32
CONV_ROW_CHUNK = 128
CONV_LANE_CHUNK = 128
LN_ROW_CHUNK = 64
ATTN_UNROLL = 4
FFN_ROWS = 512
FFN_COLS = 1024

LOG2E = float(np.log2(np.e))
Q_SCALE = HEAD_DIM ** -0.5 * LOG2E
ALIBI_PARTS = 3
V_PAD_ROWS = 16
MASKED = -1e30
NT_DIMS = (((1,), (1,)), ((), ()))


def _resident(shape):
    return pl.BlockSpec(shape, lambda *_: (0,) * len(shape), pipeline_mode=pl.Buffered(1))


def _alibi_slopes(n_heads):
    return (2.0 ** (-8.0 * np.arange(1, n_heads + 1) / n_heads)).astype(np.float32)


def _in_proj_kernel(x_ref, g_ref, wk_ref, wqvt_ref, wglu_ref, bglu_ref,
                    k_ref, qvt_ref, kmean_ref, h_ref, *, scale):
    rows = x_ref.shape[1]
    conv_ch = h_ref.shape[2]
    x = x_ref[0]
    ms = jnp.mean(x * x, axis=-1, keepdims=True)
    a = (x * lax.rsqrt(ms + RMS_EPS) * g_ref[...]).astype(BF16)

    for c in range(0, ATTN_WIDTH, IN_PROJ_COLS):
        r = jnp.dot(a, wk_ref[:, c:c + IN_PROJ_COLS], preferred_element_type=F32)
        blocks = r.reshape(rows // MOBA_BLOCK, MOBA_BLOCK, IN_PROJ_COLS)
        kmean_ref[0, 0, :, c:c + IN_PROJ_COLS] = jnp.mean(blocks, axis=1)
        k_ref[0, :, c:c + IN_PROJ_COLS] = r.astype(BF16)

    for rc in range(0, 2 * ATTN_WIDTH, MOBA_BLOCK):
        r = lax.dot_general(wqvt_ref[rc:rc + MOBA_BLOCK, :], a, NT_DIMS,
                            preferred_element_type=F32)
        if rc < ATTN_WIDTH:
            r = r * scale
        r = r.astype(BF16)
        for blk in range(rows // MOBA_BLOCK):
            qvt_ref[0, blk, rc:rc + MOBA_BLOCK, :] = r[:, blk * MOBA_BLOCK:(blk + 1) * MOBA_BLOCK]

    for c in range(0, conv_ch, IN_PROJ_COLS):
        val = jnp.dot(a, wglu_ref[:, c:c + IN_PROJ_COLS], preferred_element_type=F32)
        val = val + bglu_ref[:, c:c + IN_PROJ_COLS]
        gt = jnp.dot(a, wglu_ref[:, conv_ch + c:conv_ch + c + IN_PROJ_COLS], preferred_element_type=F32)
        gt = gt + bglu_ref[:, conv_ch + c:conv_ch + c + IN_PROJ_COLS]
        h_ref[0, :, c:c + IN_PROJ_COLS] = val * (1.0 / (1.0 + jnp.exp(-gt)))


def _in_proj(x, g, w_k, w_qvt, w_glu, b_glu):
    B, S, D = x.shape
    conv_ch = w_glu.shape[1] // 2
    rows = IN_PROJ_ROWS
    n_blk = rows // MOBA_BLOCK
    grid = (B, S // rows)
    out_shape = (
        jax.ShapeDtypeStruct((B, S, ATTN_WIDTH), BF16),
        jax.ShapeDtypeStruct((B, S // MOBA_BLOCK, 2 * ATTN_WIDTH, MOBA_BLOCK), BF16),
        jax.ShapeDtypeStruct((B, S // rows, n_blk, ATTN_WIDTH), F32),
        jax.ShapeDtypeStruct((B, S, conv_ch), F32),
    )
    return pl.pallas_call(
        functools.partial(_in_proj_kernel, scale=Q_SCALE),
        grid=grid,
        in_specs=[
            pl.BlockSpec((1, rows, D), lambda b, i: (b, i, 0)),
            _resident((1, D)),
            _resident(w_k.shape),
            _resident(w_qvt.shape),
            _resident(w_glu.shape),
            _resident((1, 2 * conv_ch)),
        ],
        out_specs=(
            pl.BlockSpec((1, rows, ATTN_WIDTH), lambda b, i: (b, i, 0)),
            pl.BlockSpec((1, n_blk, 2 * ATTN_WIDTH, MOBA_BLOCK), lambda b, i: (b, i, 0, 0)),
            pl.BlockSpec((1, 1, n_blk, ATTN_WIDTH), lambda b, i: (b, i, 0, 0)),
            pl.BlockSpec((1, rows, conv_ch), lambda b, i: (b, i, 0)),
        ),
        out_shape=out_shape,
        compiler_params=pltpu.CompilerParams(
            dimension_semantics=("arbitrary", "arbitrary"),
            vmem_limit_bytes=V7X_VMEM_LIMIT_BYTES),
        name="in_proj",
    )(x, g, w_k, w_qvt, w_glu, b_glu)


def _pair_schedule(n_blocks, unroll):
    todo = {qi: list(range(qi)) for qi in range(1, n_blocks)}
    key_blocks, query_tiles = [], []
    while any(todo.values()):
        tiles = sorted((qi for qi in todo if todo[qi]), key=lambda t: -len(todo[t]))[:unroll]
        for qi in tiles:
            key_blocks.append(todo[qi].pop(0))
            query_tiles.append(qi)
        for _ in range(unroll - len(tiles)):
            key_blocks.append(0)
            query_tiles.append(n_blocks)
    if (len(key_blocks) // unroll) % 2:
        key_blocks += [0] * unroll
        query_tiles += [n_blocks] * unroll
    return np.asarray(key_blocks, np.int32), np.asarray(query_tiles, np.int32)


def _moba_kernel(slopes_ref, pair_k_ref, pair_q_ref, qext_ref, qt_ref, k_ref, vt_ref, kmean_ref, o_ref,
                 qaug_ref, kaug_ref, vaug_ref, pen_ref, m_ref, acc_ref, s_even_ref, s_odd_ref):
    L = MOBA_BLOCK
    U = ATTN_UNROLL
    Dh = HEAD_DIM
    n_blocks = vt_ref.shape[1]
    n_steps = pair_k_ref.shape[0] // U
    slope = slopes_ref[pl.program_id(1)]

    lane = lax.broadcasted_iota(jnp.int32, (L, Dh), 1)
    key_cols = jnp.where(lane < ALIBI_PARTS, lax.broadcasted_iota(jnp.int32, (L, Dh), 0), 0).astype(BF16)
    ones_row = (lax.broadcasted_iota(jnp.int32, (V_PAD_ROWS, L), 0) == 0).astype(BF16)
    for blk in range(n_blocks):
        kaug_ref[blk * L:(blk + 1) * L, 0:Dh] = k_ref[0, blk * L:(blk + 1) * L, :]
        kaug_ref[blk * L:(blk + 1) * L, Dh:2 * Dh] = key_cols
        qaug_ref[blk, 0:Dh, :] = qt_ref[0, blk]
        qaug_ref[blk, Dh:2 * Dh, :] = qext_ref[0]
        vaug_ref[blk, 0:Dh, :] = vt_ref[0, blk]
        vaug_ref[blk, Dh:Dh + V_PAD_ROWS, :] = ones_row

    key_local = lax.broadcasted_iota(jnp.int32, (L, L), 0)
    qry_local = lax.broadcasted_iota(jnp.int32, (L, L), 1)
    kmean = kmean_ref[0].astype(BF16)
    blk_id = lax.broadcasted_iota(jnp.int32, (n_blocks, L), 0).astype(F32)

    def init_tiles(g, carry):
        tiles = [g * U + u for u in range(U)]
        for u, qi in enumerate(tiles):
            k_own = kaug_ref[pl.ds(pl.multiple_of(qi * L, L), L), :]
            s_even_ref[u] = jnp.dot(k_own, qaug_ref[qi], preferred_element_type=F32)
        gates = [jnp.dot(kmean, qt_ref[0, qi], preferred_element_type=F32) for qi in tiles]
        for u, qi in enumerate(tiles):
            gate = jnp.where(blk_id < qi.astype(F32), gates[u], -jnp.inf)
            sel = jnp.zeros((n_blocks, L), F32)
            for r in range(MOBA_TOPK):
                best = jnp.max(gate, axis=0, keepdims=True)
                first = jnp.min(jnp.where(gate == best, blk_id, float(n_blocks)), axis=0, keepdims=True)
                hit = blk_id == first
                sel = jnp.maximum(sel, jnp.where(hit, (qi > r).astype(F32), 0.0))
                gate = jnp.where(hit, -jnp.inf, gate)
            pen_ref[qi] = jnp.where(sel > 0.0, 0.0, MASKED)

            s = jnp.where(key_local <= qry_local, s_even_ref[u], -jnp.inf)
            m = jnp.max(s, axis=0, keepdims=True)
            p = jnp.exp2(s - m)
            m_ref[qi] = m
            acc_ref[qi] = jnp.dot(vaug_ref[qi], p.astype(BF16), preferred_element_type=F32)
        return carry

    lax.fori_loop(0, n_blocks // U, init_tiles, 0)
    pen_ref[n_blocks] = jnp.full((n_blocks, L), MASKED, F32)
    m_ref[n_blocks] = jnp.zeros((1, L), F32)
    acc_ref[n_blocks] = jnp.zeros(acc_ref.shape[1:], F32)

    def scores(step, s_ref):
        for u in range(U):
            j = pair_k_ref[step * U + u]
            qi = jnp.minimum(pair_q_ref[step * U + u], n_blocks - 1)
            k_blk = kaug_ref[pl.ds(pl.multiple_of(j * L, L), L), :]
            s_ref[u] = jnp.dot(k_blk, qaug_ref[qi], preferred_element_type=F32)

    def update(step, s_ref):
        loaded = []
        for u in range(U):
            j = pair_k_ref[step * U + u]
            slot = pair_q_ref[step * U + u]
            row = pen_ref[slot, pl.ds(j, 1), :] - slope * float(L) * (slot - j).astype(F32)
            loaded.append((u, j, slot, row, m_ref[slot], acc_ref[slot]))
        updated = []
        for u, j, slot, row, m, acc in loaded:
            s = s_ref[u]
            m_new = jnp.maximum(m, jnp.max(s, axis=0, keepdims=True) + row)
            alpha = jnp.exp2(m - m_new)
            p = jnp.exp2(s + (row - m_new))
            acc = alpha * acc + jnp.dot(vaug_ref[j], p.astype(BF16), preferred_element_type=F32)
            updated.append((slot, m_new, acc))
        for slot, m, acc in updated:
            m_ref[slot] = m
            acc_ref[slot] = acc

    def two_steps(t, carry):
        step = 2 * t
        scores(step + 1, s_odd_ref)
        update(step, s_even_ref)
        scores(jnp.minimum(step + 2, n_steps - 1), s_even_ref)
        update(step + 1, s_odd_ref)
        return carry

    scores(0, s_even_ref)
    lax.fori_loop(0, n_steps // 2, two_steps, 0)

    def finish_tiles(g, carry):
        for u in range(U):
            qi = g * U + u
            q0 = pl.multiple_of(qi * L, L)
            acc = acc_ref[qi]
            o_ref[0, pl.ds(q0, L), :] = (acc[0:Dh] / acc[Dh:Dh + 1]).T.astype(o_ref.dtype)
        return carry

    lax.fori_loop(0, n_blocks // U, finish_tiles, 0)


def _bf16_parts(x, n):
    parts, rest = [], np.asarray(x, np.float32)
    for _ in range(n):
        part = rest.astype(BF16).astype(np.float32)
        parts.append(part)
        rest = (rest - part).astype(np.float32)
    return np.stack(parts, axis=-1)


def _moba(k, qvt, kmean):
    B, S, _ = k.shape
    n_blocks = S // MOBA_BLOCK
    assert n_blocks % ATTN_UNROLL == 0
    pair_k, pair_q = _pair_schedule(n_blocks, ATTN_UNROLL)
    slopes = (_alibi_slopes(N_HEADS) * LOG2E).astype(np.float32)
    qext = np.zeros((N_HEADS, HEAD_DIM, MOBA_BLOCK), np.float32)
    qext[:, :ALIBI_PARTS, :] = _bf16_parts(slopes, ALIBI_PARTS)[:, :, None]
    smem = pl.BlockSpec(memory_space=pltpu.SMEM)
    return pl.pallas_call(
        _moba_kernel,
        grid=(B, N_HEADS),
        in_specs=[
            smem, smem, smem,
            pl.BlockSpec((1, HEAD_DIM, MOBA_BLOCK), lambda b, h: (h, 0, 0)),
            pl.BlockSpec((1, n_blocks, HEAD_DIM, MOBA_BLOCK), lambda b, h: (b, 0, h, 0)),
            pl.BlockSpec((1, S, HEAD_DIM), lambda b, h: (b, 0, h)),
            pl.BlockSpec((1, n_blocks, HEAD_DIM, MOBA_BLOCK), lambda b, h: (b, 0, N_HEADS + h, 0)),
            pl.BlockSpec((1, n_blocks, HEAD_DIM), lambda b, h: (b, 0, h)),
        ],
        out_specs=pl.BlockSpec((1, S, HEAD_DIM), lambda b, h: (b, 0, h)),
        out_shape=jax.ShapeDtypeStruct((B, S, ATTN_WIDTH), BF16),
        scratch_shapes=[
            pltpu.VMEM((n_blocks, 2 * HEAD_DIM, MOBA_BLOCK), BF16),
            pltpu.VMEM((S, 2 * HEAD_DIM), BF16),
            pltpu.VMEM((n_blocks, HEAD_DIM + V_PAD_ROWS, MOBA_BLOCK), BF16),
            pltpu.VMEM((n_blocks + 1, n_blocks, MOBA_BLOCK), F32),
            pltpu.VMEM((n_blocks + 1, 1, MOBA_BLOCK), F32),
            pltpu.VMEM((n_blocks + 1, HEAD_DIM + V_PAD_ROWS, MOBA_BLOCK), F32),
            pltpu.VMEM((ATTN_UNROLL, MOBA_BLOCK, MOBA_BLOCK), F32),
            pltpu.VMEM((ATTN_UNROLL, MOBA_BLOCK, MOBA_BLOCK), F32),
        ],
        compiler_params=pltpu.CompilerParams(
            dimension_semantics=("arbitrary", "arbitrary"),
            vmem_limit_bytes=V7X_VMEM_LIMIT_BYTES),
        name="moba",
    )(jnp.asarray(slopes), jnp.asarray(pair_k), jnp.asarray(pair_q), jnp.asarray(qext, BF16),
      qvt, k, qvt, kmean)


def _mix_kernel(x_ref, attn_ref, hmain_ref, hhalo_ref, wdw_ref, bdw_ref, lng_ref, lnb_ref,
                wo_ref, gpost_ref, gpre_ref, h1_ref, f_ref, ext_ref, y_ref, conv_ref):
    rows = x_ref.shape[1]
    conv_ch = hmain_ref.shape[2]
    first_tile = pl.program_id(1) == 0

    ext_ref[0:CONV_HALO, :] = jnp.where(first_tile, 0.0, hhalo_ref[0])
    ext_ref[CONV_HALO:CONV_HALO + rows, :] = hmain_ref[0]

    shift0 = CONV_HALO - (CONV_WIDTH - 1)
    R, CL = CONV_ROW_CHUNK, CONV_LANE_CHUNK
    for r0 in range(0, rows, R):
        for c0 in range(0, conv_ch, CL):
            y = None
            for b in range(SUBLANES):
                z_rows = R if b == 0 else R + SUBLANES
                z = None
                for a in range((shift0 + CONV_WIDTH - 1) // SUBLANES + 1):
                    w = SUBLANES * a + b - shift0
                    if 0 <= w < CONV_WIDTH:
                        lo = r0 + SUBLANES * a
                        tap = ext_ref[lo:lo + z_rows, c0:c0 + CL] * wdw_ref[w:w + 1, c0:c0 + CL]
                        z = tap if z is None else z + tap
                part = z[b:b + R]
                y = part if y is None else y + part
            y_ref[r0:r0 + R, c0:c0 + CL] = y + bdw_ref[:, c0:c0 + CL]

    for r0 in range(0, rows, LN_ROW_CHUNK):
        y = y_ref[r0:r0 + LN_ROW_CHUNK, :]
        mu = jnp.mean(y, axis=-1, keepdims=True)
        var = jnp.mean(jnp.square(y - mu), axis=-1, keepdims=True)
        y = (y - mu) * lax.rsqrt(var + LN_EPS) * lng_ref[...] + lnb_ref[...]
        y = y * (1.0 / (1.0 + jnp.exp(-y)))
        conv_ref[r0:r0 + LN_ROW_CHUNK, :] = y.astype(BF16)

    attn = attn_ref[0]
    conv = conv_ref[...]
    attn_w = attn.shape[1]
    cols = []
    for c in range(0, wo_ref.shape[1], MIX_COLS):
        r = jnp.dot(attn, wo_ref[0:attn_w, c:c + MIX_COLS], preferred_element_type=F32)
        r = r + jnp.dot(conv, wo_ref[attn_w:attn_w + conv_ch, c:c + MIX_COLS], preferred_element_type=F32)
        cols.append(r)
    mixed = jnp.concatenate(cols, axis=-1)
    ms = jnp.mean(mixed * mixed, axis=-1, keepdims=True)
    h1 = x_ref[0] + mixed * lax.rsqrt(ms + RMS_EPS) * gpost_ref[...]
    h1_ref[0] = h1
    ms1 = jnp.mean(h1 * h1, axis=-1, keepdims=True)
    f_ref[0] = (h1 * lax.rsqrt(ms1 + RMS_EPS) * gpre_ref[...]).astype(BF16)


def _mix(x, attn, hglu, w_dw, b_dw, ln_g, ln_b, w_out, g_post, g_pre):
    B, S, D = x.shape
    conv_ch = hglu.shape[2]
    rows = MIX_ROWS
    halo_per_tile = rows // CONV_HALO
    return pl.pallas_call(
        _mix_kernel,
        grid=(B, S // rows),
        in_specs=[
            pl.BlockSpec((1, rows, D), lambda b, i: (b, i, 0)),
            pl.BlockSpec((1, rows, attn.shape[2]), lambda b, i: (b, i, 0)),
            pl.BlockSpec((1, rows, conv_ch), lambda b, i: (b, i, 0)),
            pl.BlockSpec((1, CONV_HALO, conv_ch),
                         lambda b, i: (b, jnp.maximum(i * halo_per_tile - 1, 0), 0)),
            _resident(w_dw.shape),
            _resident((1, conv_ch)),
            _resident((1, conv_ch)),
            _resident((1, conv_ch)),
            _resident(w_out.shape),
            _resident((1, D)),
            _resident((1, D)),
        ],
        out_specs=(
            pl.BlockSpec((1, rows, D), lambda b, i: (b, i, 0)),
            pl.BlockSpec((1, rows, D), lambda b, i: (b, i, 0)),
        ),
        out_shape=(jax.ShapeDtypeStruct((B, S, D), F32), jax.ShapeDtypeStruct((B, S, D), BF16)),
        scratch_shapes=[
            pltpu.VMEM((CONV_HALO + rows, conv_ch), F32),
            pltpu.VMEM((rows, conv_ch), F32),
            pltpu.VMEM((rows, conv_ch), BF16),
        ],
        compiler_params=pltpu.CompilerParams(
            dimension_semantics=("arbitrary", "arbitrary"),
            vmem_limit_bytes=V7X_VMEM_LIMIT_BYTES),
        name="mix",
    )(x, attn, hglu, hglu, w_dw, b_dw, ln_g, ln_b, w_out, g_post, g_pre)


def _ffn_kernel(f_ref, h1_ref, w1_ref, w2_ref, g_ref, o_ref, acc_ref):
    kf = pl.program_id(2)

    @pl.when(kf == 0)
    def _():
        acc_ref[...] = jnp.zeros_like(acc_ref)

    u = jnp.dot(f_ref[0], w1_ref[...], preferred_element_type=F32)
    u = jnp.square(jnp.maximum(u, 0.0)).astype(BF16)
    acc_ref[...] += jnp.dot(u, w2_ref[...], preferred_element_type=F32)

    @pl.when(kf == pl.num_programs(2) - 1)
    def _():
        a = acc_ref[...]
        ms = jnp.mean(a * a, axis=-1, keepdims=True)
        o_ref[0] = h1_ref[0] + a * lax.rsqrt(ms + RMS_EPS) * g_ref[...]


def _ffn(f, h1, w1, w2, g):
    B, S, D = h1.shape
    d_ff = w1.shape[1]
    rows, cols = FFN_ROWS, FFN_COLS
    return pl.pallas_call(
        _ffn_kernel,
        grid=(B, S // rows, d_ff // cols),
        in_specs=[
            pl.BlockSpec((1, rows, D), lambda b, i, k: (b, i, 0)),
            pl.BlockSpec((1, rows, D), lambda b, i, k: (b, i, 0)),
            pl.BlockSpec((D, cols), lambda b, i, k: (0, k)),
            pl.BlockSpec((cols, D), lambda b, i, k: (k, 0)),
            _resident((1, D)),
        ],
        out_specs=pl.BlockSpec((1, rows, D), lambda b, i, k: (b, i, 0)),
        out_shape=jax.ShapeDtypeStruct((B, S, D), F32),
        scratch_shapes=[pltpu.VMEM((rows, D), F32)],
        compiler_params=pltpu.CompilerParams(
            dimension_semantics=("arbitrary", "arbitrary", "arbitrary"),
            vmem_limit_bytes=V7X_VMEM_LIMIT_BYTES),
        name="ffn",
    )(f, h1, w1, w2, g)


def kernel(x, g_mix_pre, w_in, b_glu, w_dw, b_dw, ln_conv_g, ln_conv_b, w_out, g_mix_post,
           g_ffn_pre, w_ff1, w_ff2, g_ffn_post):
    B, S, D = x.shape
    depth = w_in.shape[0]
    assert S % IN_PROJ_ROWS == 0 and S % MIX_ROWS == 0 and S % FFN_ROWS == 0
    assert IN_PROJ_ROWS % MOBA_BLOCK == 0 and MIX_ROWS % CONV_HALO == 0
    assert CONV_HALO >= CONV_WIDTH - 1 and CONV_HALO % SUBLANES == 0
    row = lambda v: v.reshape(1, -1)

    h = x
    for l in range(depth):
        w_q = w_in[l, :, :ATTN_WIDTH]
        w_k = w_in[l, :, ATTN_WIDTH:2 * ATTN_WIDTH].astype(BF16)
        w_v = w_in[l, :, 2 * ATTN_WIDTH:3 * ATTN_WIDTH]
        w_qvt = jnp.concatenate([w_q, w_v], axis=1).T.astype(BF16)
        w_glu = w_in[l, :, 3 * ATTN_WIDTH:].astype(BF16)

        k, qvt, kmean, hglu = _in_proj(h, row(g_mix_pre[l]), w_k, w_qvt, w_glu, row(b_glu[l]))
        kmean = kmean.reshape(B, S // MOBA_BLOCK, ATTN_WIDTH)
        attn = _moba(k, qvt, kmean)
        h1, f = _mix(h, attn, hglu, w_dw[l], row(b_dw[l]), row(ln_conv_g[l]), row(ln_conv_b[l]),
                     w_out[l].astype(BF16), row(g_mix_post[l]), row(g_ffn_pre[l]))
        h = _ffn(f, h1, w_ff1[l].astype(BF16), w_ff2[l].astype(BF16), row(g_ffn_post[l]))
    return h
```

```python
import functools

import numpy as np
import jax
import jax.numpy as jnp
from jax import lax
from jax.experimental import pallas as pl
from jax.experimental.pallas import tpu as pltpu

F32 = jnp.float32
BF16 = jnp.bfloat16

HEAD_DIM = 128
N_HEADS = 8
ATTN_WIDTH = N_HEADS * HEAD_DIM
MOBA_BLOCK = 256
MOBA_TOPK = 3
CONV_WIDTH = 31
RMS_EPS = 1e-6
LN_EPS = 1e-5

V7X_VMEM_LIMIT_BYTES = 60000 * 1024
SUBLANES = 8

IN_PROJ_ROWS = 512
IN_PROJ_COLS = 512
MIX_ROWS = 256
MIX_COLS = 512
CONV_HALO = 32
CONV_ROW_CHUNK = 128
CONV_LANE_CHUNK = 128
LN_ROW_CHUNK = 64
ATTN_UNROLL = 4
ATTN_BODY_STEPS = 4
FFN_ROWS = 512
FFN_COLS = 1024

LOG2E = float(np.log2(np.e))
Q_SCALE = HEAD_DIM ** -0.5 * LOG2E
ALIBI_PARTS = 3
V_PAD_ROWS = 16
MASKED = -1e30
NT_DIMS = (((1,), (1,)), ((), ()))


def _resident(shape):
    return pl.BlockSpec(shape, lambda *_: (0,) * len(shape), pipeline_mode=pl.Buffered(1))


def _alibi_slopes(n_heads):
    return (2.0 ** (-8.0 * np.arange(1, n_heads + 1) / n_heads)).astype(np.float32)


def _in_proj_kernel(x_ref, g_ref, wk_ref, wqvt_ref, wglu_ref, bglu_ref,
                    k_ref, qvt_ref, kmean_ref, h_ref, *, scale):
    rows = x_ref.shape[1]
    conv_ch = h_ref.shape[2]
    x = x_ref[0]
    ms = jnp.mean(x * x, axis=-1, keepdims=True)
    a = (x * lax.rsqrt(ms + RMS_EPS) * g_ref[...]).astype(BF16)

    for c in range(0, ATTN_WIDTH, IN_PROJ_COLS):
        r = jnp.dot(a, wk_ref[:, c:c + IN_PROJ_COLS], preferred_element_type=F32)
        blocks = r.reshape(rows // MOBA_BLOCK, MOBA_BLOCK, IN_PROJ_COLS)
        kmean_ref[0, 0, :, c:c + IN_PROJ_COLS] = jnp.mean(blocks, axis=1)
        k_ref[0, :, c:c + IN_PROJ_COLS] = r.astype(BF16)

    for rc in range(0, 2 * ATTN_WIDTH, MOBA_BLOCK):
        r = lax.dot_general(wqvt_ref[rc:rc + MOBA_BLOCK, :], a, NT_DIMS,
                            preferred_element_type=F32)
        if rc < ATTN_WIDTH:
            r = r * scale
        r = r.astype(BF16)
        for blk in range(rows // MOBA_BLOCK):
            qvt_ref[0, blk, rc:rc + MOBA_BLOCK, :] = r[:, blk * MOBA_BLOCK:(blk + 1) * MOBA_BLOCK]

    for c in range(0, conv_ch, IN_PROJ_COLS):
        val = jnp.dot(a, wglu_ref[:, c:c + IN_PROJ_COLS], preferred_element_type=F32)
        val = val + bglu_ref[:, c:c + IN_PROJ_COLS]
        gt = jnp.dot(a, wglu_ref[:, conv_ch + c:conv_ch + c + IN_PROJ_COLS], preferred_element_type=F32)
        gt = gt + bglu_ref[:, conv_ch + c:conv_ch + c + IN_PROJ_COLS]
        h_ref[0, :, c:c + IN_PROJ_COLS] = val * (1.0 / (1.0 + jnp.exp(-gt)))


def _in_proj(x, g, w_k, w_qvt, w_glu, b_glu):
    B, S, D = x.shape
    conv_ch = w_glu.shape[1] // 2
    rows = IN_PROJ_ROWS
    n_blk = rows // MOBA_BLOCK
    grid = (B, S // rows)
    out_shape = (
        jax.ShapeDtypeStruct((B, S, ATTN_WIDTH), BF16),
        jax.ShapeDtypeStruct((B, S // MOBA_BLOCK, 2 * ATTN_WIDTH, MOBA_BLOCK), BF16),
        jax.ShapeDtypeStruct((B, S // rows, n_blk, ATTN_WIDTH), F32),
        jax.ShapeDtypeStruct((B, S, conv_ch), F32),
    )
    return pl.pallas_call(
        functools.partial(_in_proj_kernel, scale=Q_SCALE),
        grid=grid,
        in_specs=[
            pl.BlockSpec((1, rows, D), lambda b, i: (b, i, 0)),
            _resident((1, D)),
            _resident(w_k.shape),
            _resident(w_qvt.shape),
            _resident(w_glu.shape),
            _resident((1, 2 * conv_ch)),
        ],
        out_specs=(
            pl.BlockSpec((1, rows, ATTN_WIDTH), lambda b, i: (b, i, 0)),
            pl.BlockSpec((1, n_blk, 2 * ATTN_WIDTH, MOBA_BLOCK), lambda b, i: (b, i, 0, 0)),
            pl.BlockSpec((1, 1, n_blk, ATTN_WIDTH), lambda b, i: (b, i, 0, 0)),
            pl.BlockSpec((1, rows, conv_ch), lambda b, i: (b, i, 0)),
        ),
        out_shape=out_shape,
        compiler_params=pltpu.CompilerParams(
            dimension_semantics=("arbitrary", "arbitrary"),
            vmem_limit_bytes=V7X_VMEM_LIMIT_BYTES),
        name="in_proj",
    )(x, g, w_k, w_qvt, w_glu, b_glu)


def _pair_schedule(n_blocks, unroll):
    todo = {qi: list(range(qi)) for qi in range(1, n_blocks)}
    key_blocks, query_tiles = [], []
    while any(todo.values()):
        tiles = sorted((qi for qi in todo if todo[qi]), key=lambda t: -len(todo[t]))[:unroll]
        for qi in tiles:
            key_blocks.append(todo[qi].pop(0))
            query_tiles.append(qi)
        for _ in range(unroll - len(tiles)):
            key_blocks.append(0)
            query_tiles.append(n_blocks)
    while (len(key_blocks) // unroll) % ATTN_BODY_STEPS:
        key_blocks += [0] * unroll
        query_tiles += [n_blocks] * unroll
    return np.asarray(key_blocks, np.int32), np.asarray(query_tiles, np.int32)


def _moba_kernel(slopes_ref, pair_k_ref, pair_q_ref, qext_ref, qt_ref, k_ref, vt_ref, kmean_ref, o_ref,
                 qaug_ref, kaug_ref, vaug_ref, pen_ref, m_ref, acc_ref, s_even_ref, s_odd_ref):
    L = MOBA_BLOCK
    U = ATTN_UNROLL
    Dh = HEAD_DIM
    n_blocks = vt_ref.shape[1]
    n_steps = pair_k_ref.shape[0] // U
    slope = slopes_ref[pl.program_id(1)]

    lane = lax.broadcasted_iota(jnp.int32, (L, Dh), 1)
    key_cols = jnp.where(lane < ALIBI_PARTS, lax.broadcasted_iota(jnp.int32, (L, Dh), 0), 0).astype(BF16)
    ones_row = (lax.broadcasted_iota(jnp.int32, (V_PAD_ROWS, L), 0) == 0).astype(BF16)
    for blk in range(n_blocks):
        kaug_ref[blk * L:(blk + 1) * L, 0:Dh] = k_ref[0, blk * L:(blk + 1) * L, :]
        kaug_ref[blk * L:(blk + 1) * L, Dh:2 * Dh] = key_cols
        qaug_ref[blk, 0:Dh, :] = qt_ref[0, blk]
        qaug_ref[blk, Dh:2 * Dh, :] = qext_ref[0]
        vaug_ref[blk, 0:Dh, :] = vt_ref[0, blk]
        vaug_ref[blk, Dh:Dh + V_PAD_ROWS, :] = ones_row

    key_local = lax.broadcasted_iota(jnp.int32, (L, L), 0)
    qry_local = lax.broadcasted_iota(jnp.int32, (L, L), 1)
    kmean = kmean_ref[0].astype(BF16)
    blk_id = lax.broadcasted_iota(jnp.int32, (n_blocks, L), 0).astype(F32)

    def init_tiles(g, carry):
        tiles = [g * U + u for u in range(U)]
        for u, qi in enumerate(tiles):
            k_own = kaug_ref[pl.ds(pl.multiple_of(qi * L, L), L), :]
            s_even_ref[u] = jnp.dot(k_own, qaug_ref[qi], preferred_element_type=F32)
        gates = [jnp.dot(kmean, qt_ref[0, qi], preferred_element_type=F32) for qi in tiles]
        for u, qi in enumerate(tiles):
            gate = jnp.where(blk_id < qi.astype(F32), gates[u], -jnp.inf)
            sel = jnp.zeros((n_blocks, L), F32)
            for r in range(MOBA_TOPK):
                best = jnp.max(gate, axis=0, keepdims=True)
                first = jnp.min(jnp.where(gate == best, blk_id, float(n_blocks)), axis=0, keepdims=True)
                hit = blk_id == first
                sel = jnp.maximum(sel, jnp.where(hit, (qi > r).astype(F32), 0.0))
                gate = jnp.where(hit, -jnp.inf, gate)
            pen_ref[qi] = jnp.where(sel > 0.0, 0.0, MASKED)

            s = jnp.where(key_local <= qry_local, s_even_ref[u], -jnp.inf)
            m = jnp.max(s, axis=0, keepdims=True)
            p = jnp.exp2(s - m)
            m_ref[qi] = m
            acc_ref[qi] = jnp.dot(vaug_ref[qi], p.astype(BF16), preferred_element_type=F32)
        return carry

    lax.fori_loop(0, n_blocks // U, init_tiles, 0)
    pen_ref[n_blocks] = jnp.full((n_blocks, L), MASKED, F32)
    m_ref[n_blocks] = jnp.zeros((1, L), F32)
    acc_ref[n_blocks] = jnp.zeros(acc_ref.shape[1:], F32)

    def scores(step, s_ref):
        for u in range(U):
            j = pair_k_ref[step * U + u]
            qi = jnp.minimum(pair_q_ref[step * U + u], n_blocks - 1)
            k_blk = kaug_ref[pl.ds(pl.multiple_of(j * L, L), L), :]
            s_ref[u] = jnp.dot(k_blk, qaug_ref[qi], preferred_element_type=F32)

    def update(step, s_ref):
        loaded = []
        for u in range(U):
            j = pair_k_ref[step * U + u]
            slot = pair_q_ref[step * U + u]
            row = pen_ref[slot, pl.ds(j, 1), :] - slope * float(L) * (slot - j).astype(F32)
            loaded.append((u, j, slot, row, m_ref[slot]))
        updated = []
        for u, j, slot, row, m in loaded:
            s = s_ref[u]
            m_new = jnp.maximum(m, jnp.max(s, axis=0, keepdims=True) + row)
            alpha = jnp.exp2(m - m_new)
            p = jnp.exp2(s + (row - m_new))
            pv = jnp.dot(vaug_ref[j], p.astype(BF16), preferred_element_type=F32)
            acc_ref[slot] = alpha * acc_ref[slot] + pv
            updated.append((slot, m_new))
        for slot, m in updated:
            m_ref[slot] = m

    def several_steps(t, carry):
        for i in range(ATTN_BODY_STEPS):
            step = ATTN_BODY_STEPS * t + i
            cur, nxt = (s_even_ref, s_odd_ref) if i % 2 == 0 else (s_odd_ref, s_even_ref)
            scores(jnp.minimum(step + 1, n_steps - 1), nxt)
            update(step, cur)
        return carry

    scores(0, s_even_ref)
    lax.fori_loop(0, n_steps // ATTN_BODY_STEPS, several_steps, 0)

    def finish_tiles(g, carry):
        for u in range(U):
            qi = g * U + u
            q0 = pl.multiple_of(qi * L, L)
            acc = acc_ref[qi]
            o_ref[0, pl.ds(q0, L), :] = (acc[0:Dh] / acc[Dh:Dh + 1]).T.astype(o_ref.dtype)
        return carry

    lax.fori_loop(0, n_blocks // U, finish_tiles, 0)


def _bf16_parts(x, n):
    parts, rest = [], np.asarray(x, np.float32)
    for _ in range(n):
        part = rest.astype(BF16).astype(np.float32)
        parts.append(part)
        rest = (rest - part).astype(np.float32)
    return np.stack(parts, axis=-1)


def _moba(k, qvt, kmean):
    B, S, _ = k.shape
    n_blocks = S // MOBA_BLOCK
    assert n_blocks % ATTN_UNROLL == 0 and ATTN_BODY_STEPS % 2 == 0
    pair_k, pair_q = _pair_schedule(n_blocks, ATTN_UNROLL)
    slopes = (_alibi_slopes(N_HEADS) * LOG2E).astype(np.float32)
    qext = np.zeros((N_HEADS, HEAD_DIM, MOBA_BLOCK), np.float32)
    qext[:, :ALIBI_PARTS, :] = _bf16_parts(slopes, ALIBI_PARTS)[:, :, None]
    smem = pl.BlockSpec(memory_space=pltpu.SMEM)
    return pl.pallas_call(
        _moba_kernel,
        grid=(B, N_HEADS),
        in_specs=[
            smem, smem, smem,
            pl.BlockSpec((1, HEAD_DIM, MOBA_BLOCK), lambda b, h: (h, 0, 0)),
            pl.BlockSpec((1, n_blocks, HEAD_DIM, MOBA_BLOCK), lambda b, h: (b, 0, h, 0)),
            pl.BlockSpec((1, S, HEAD_DIM), lambda b, h: (b, 0, h)),
            pl.BlockSpec((1, n_blocks, HEAD_DIM, MOBA_BLOCK), lambda b, h: (b, 0, N_HEADS + h, 0)),
            pl.BlockSpec((1, n_blocks, HEAD_DIM), lambda b, h: (b, 0, h)),
        ],
        out_specs=pl.BlockSpec((1, S, HEAD_DIM), lambda b, h: (b, 0, h)),
        out_shape=jax.ShapeDtypeStruct((B, S, ATTN_WIDTH), BF16),
        scratch_shapes=[
            pltpu.VMEM((n_blocks, 2 * HEAD_DIM, MOBA_BLOCK), BF16),
            pltpu.VMEM((S, 2 * HEAD_DIM), BF16),
            pltpu.VMEM((n_blocks, HEAD_DIM + V_PAD_ROWS, MOBA_BLOCK), BF16),
            pltpu.VMEM((n_blocks + 1, n_blocks, MOBA_BLOCK), F32),
            pltpu.VMEM((n_blocks + 1, 1, MOBA_BLOCK), F32),
            pltpu.VMEM((n_blocks + 1, HEAD_DIM + V_PAD_ROWS, MOBA_BLOCK), F32),
            pltpu.VMEM((ATTN_UNROLL, MOBA_BLOCK, MOBA_BLOCK), F32),
            pltpu.VMEM((ATTN_UNROLL, MOBA_BLOCK, MOBA_BLOCK), F32),
        ],
        compiler_params=pltpu.CompilerParams(
            dimension_semantics=("arbitrary", "arbitrary"),
            vmem_limit_bytes=V7X_VMEM_LIMIT_BYTES),
        name="moba",
    )(jnp.asarray(slopes), jnp.asarray(pair_k), jnp.asarray(pair_q), jnp.asarray(qext, BF16),
      qvt, k, qvt, kmean)


def _mix_kernel(x_ref, attn_ref, hmain_ref, hhalo_ref, wdw_ref, bdw_ref, lng_ref, lnb_ref,
                wo_ref, gpost_ref, gpre_ref, h1_ref, f_ref, ext_ref, y_ref, conv_ref):
    rows = x_ref.shape[1]
    conv_ch = hmain_ref.shape[2]
    first_tile = pl.program_id(1) == 0

    ext_ref[0:CONV_HALO, :] = jnp.where(first_tile, 0.0, hhalo_ref[0])
    ext_ref[CONV_HALO:CONV_HALO + rows, :] = hmain_ref[0]

    shift0 = CONV_HALO - (CONV_WIDTH - 1)
    R, CL = CONV_ROW_CHUNK, CONV_LANE_CHUNK
    for r0 in range(0, rows, R):
        for c0 in range(0, conv_ch, CL):
            y = None
            for b in range(SUBLANES):
                z_rows = R if b == 0 else R + SUBLANES
                z = None
                for a in range((shift0 + CONV_WIDTH - 1) // SUBLANES + 1):
                    w = SUBLANES * a + b - shift0
                    if 0 <= w < CONV_WIDTH:
                        lo = r0 + SUBLANES * a
                        tap = ext_ref[lo:lo + z_rows, c0:c0 + CL] * wdw_ref[w:w + 1, c0:c0 + CL]
                        z = tap if z is None else z + tap
                part = z[b:b + R]
                y = part if y is None else y + part
            y_ref[r0:r0 + R, c0:c0 + CL] = y + bdw_ref[:, c0:c0 + CL]

    for r0 in range(0, rows, LN_ROW_CHUNK):
        y = y_ref[r0:r0 + LN_ROW_CHUNK, :]
        mu = jnp.mean(y, axis=-1, keepdims=True)
        var = jnp.mean(jnp.square(y - mu), axis=-1, keepdims=True)
        y = (y - mu) * lax.rsqrt(var + LN_EPS) * lng_ref[...] + lnb_ref[...]
        y = y * (1.0 / (1.0 + jnp.exp(-y)))
        conv_ref[r0:r0 + LN_ROW_CHUNK, :] = y.astype(BF16)

    attn = attn_ref[0]
    conv = conv_ref[...]
    attn_w = attn.shape[1]
    cols = []
    for c in range(0, wo_ref.shape[1], MIX_COLS):
        r = jnp.dot(attn, wo_ref[0:attn_w, c:c + MIX_COLS], preferred_element_type=F32)
        r = r + jnp.dot(conv, wo_ref[attn_w:attn_w + conv_ch, c:c + MIX_COLS], preferred_element_type=F32)
        cols.append(r)
    mixed = jnp.concatenate(cols, axis=-1)
    ms = jnp.mean(mixed * mixed, axis=-1, keepdims=True)
    h1 = x_ref[0] + mixed * lax.rsqrt(ms + RMS_EPS) * gpost_ref[...]
    h1_ref[0] = h1
    ms1 = jnp.mean(h1 * h1, axis=-1, keepdims=True)
    f_ref[0] = (h1 * lax.rsqrt(ms1 + RMS_EPS) * gpre_ref[...]).astype(BF16)


def _mix(x, attn, hglu, w_dw, b_dw, ln_g, ln_b, w_out, g_post, g_pre):
    B, S, D = x.shape
    conv_ch = hglu.shape[2]
    rows = MIX_ROWS
    halo_per_tile = rows // CONV_HALO
    return pl.pallas_call(
        _mix_kernel,
        grid=(B, S // rows),
        in_specs=[
            pl.BlockSpec((1, rows, D), lambda b, i: (b, i, 0)),
            pl.BlockSpec((1, rows, attn.shape[2]), lambda b, i: (b, i, 0)),
            pl.BlockSpec((1, rows, conv_ch), lambda b, i: (b, i, 0)),
            pl.BlockSpec((1, CONV_HALO, conv_ch),
                         lambda b, i: (b, jnp.maximum(i * halo_per_tile - 1, 0), 0)),
            _resident(w_dw.shape),
            _resident((1, conv_ch)),
            _resident((1, conv_ch)),
            _resident((1, conv_ch)),
            _resident(w_out.shape),
            _resident((1, D)),
            _resident((1, D)),
        ],
        out_specs=(
            pl.BlockSpec((1, rows, D), lambda b, i: (b, i, 0)),
            pl.BlockSpec((1, rows, D), lambda b, i: (b, i, 0)),
        ),
        out_shape=(jax.ShapeDtypeStruct((B, S, D), F32), jax.ShapeDtypeStruct((B, S, D), BF16)),
        scratch_shapes=[
            pltpu.VMEM((CONV_HALO + rows, conv_ch), F32),
            pltpu.VMEM((rows, conv_ch), F32),
            pltpu.VMEM((rows, conv_ch), BF16),
        ],
        compiler_params=pltpu.CompilerParams(
            dimension_semantics=("arbitrary", "arbitrary"),
            vmem_limit_bytes=V7X_VMEM_LIMIT_BYTES),
        name="mix",
    )(x, attn, hglu, hglu, w_dw, b_dw, ln_g, ln_b, w_out, g_post, g_pre)


def _ffn_kernel(f_ref, h1_ref, w1_ref, w2_ref, g_ref, o_ref, acc_ref):
    kf = pl.program_id(2)

    @pl.when(kf == 0)
    def _():
        acc_ref[...] = jnp.zeros_like(acc_ref)

    u = jnp.dot(f_ref[0], w1_ref[...], preferred_element_type=F32)
    u = jnp.square(jnp.maximum(u, 0.0)).astype(BF16)
    acc_ref[...] += jnp.dot(u, w2_ref[...], preferred_element_type=F32)

    @pl.when(kf == pl.num_programs(2) - 1)
    def _():
        a = acc_ref[...]
        ms = jnp.mean(a * a, axis=-1, keepdims=True)
        o_ref[0] = h1_ref[0] + a * lax.rsqrt(ms + RMS_EPS) * g_ref[...]


def _ffn(f, h1, w1, w2, g):
    B, S, D = h1.shape
    d_ff = w1.shape[1]
    rows, cols = FFN_ROWS, FFN_COLS
    return pl.pallas_call(
        _ffn_kernel,
        grid=(B, S // rows, d_ff // cols),
        in_specs=[
            pl.BlockSpec((1, rows, D), lambda b, i, k: (b, i, 0)),
            pl.BlockSpec((1, rows, D), lambda b, i, k: (b, i, 0)),
            pl.BlockSpec((D, cols), lambda b, i, k: (0, k)),
            pl.BlockSpec((cols, D), lambda b, i, k: (k, 0)),
            _resident((1, D)),
        ],
        out_specs=pl.BlockSpec((1, rows, D), lambda b, i, k: (b, i, 0)),
        out_shape=jax.ShapeDtypeStruct((B, S, D), F32),
        scratch_shapes=[pltpu.VMEM((rows, D), F32)],
        compiler_params=pltpu.CompilerParams(
            dimension_semantics=("arbitrary", "arbitrary", "arbitrary"),
            vmem_limit_bytes=V7X_VMEM_LIMIT_BYTES),
        name="ffn",
    )(f, h1, w1, w2, g)


def kernel(x, g_mix_pre, w_in, b_glu, w_dw, b_dw, ln_conv_g, ln_conv_b, w_out, g_mix_post,
           g_ffn_pre, w_ff1, w_ff2, g_ffn_post):
    B, S, D = x.shape
    depth = w_in.shape[0]
    assert S % IN_PROJ_ROWS == 0 and S % MIX_ROWS == 0 and S % FFN_ROWS == 0
    assert IN_PROJ_ROWS % MOBA_BLOCK == 0 and MIX_ROWS % CONV_HALO == 0
    assert CONV_HALO >= CONV_WIDTH - 1 and CONV_HALO % SUBLANES == 0
    row = lambda v: v.reshape(1, -1)

    h = x
    for l in range(depth):
        w_q = w_in[l, :, :ATTN_WIDTH]
        w_k = w_in[l, :, ATTN_WIDTH:2 * ATTN_WIDTH].astype(BF16)
        w_v = w_in[l, :, 2 * ATTN_WIDTH:3 * ATTN_WIDTH]
        w_qvt = jnp.concatenate([w_q, w_v], axis=1).T.astype(BF16)
        w_glu = w_in[l, :, 3 * ATTN_WIDTH:].astype(BF16)

        k, qvt, kmean, hglu = _in_proj(h, row(g_mix_pre[l]), w_k, w_qvt, w_glu, row(b_glu[l]))
        kmean = kmean.reshape(B, S // MOBA_BLOCK, ATTN_WIDTH)
        attn = _moba(k, qvt, kmean)
        h1, f = _mix(h, attn, hglu, w_dw[l], row(b_dw[l]), row(ln_conv_g[l]), row(ln_conv_b[l]),
                     w_out[l].astype(BF16), row(g_mix_post[l]), row(g_ffn_pre[l]))
        h = _ffn(f, h1, w_ff1[l].astype(BF16), w_ff2[l].astype(BF16), row(g_ffn_post[l]))
    return h
```

```python
import functools

import numpy as np
import jax
import jax.numpy as jnp
from jax import lax
from jax.experimental import pallas as pl
from jax.experimental.pallas import tpu as pltpu

F32 = jnp.float32
BF16 = jnp.bfloat16

HEAD_DIM = 128
N_HEADS = 8
ATTN_WIDTH = N_HEADS * HEAD_DIM
MOBA_BLOCK = 256
MOBA_TOPK = 3
CONV_WIDTH = 31
RMS_EPS = 1e-6
LN_EPS = 1e-5

V7X_VMEM_LIMIT_BYTES = 60000 * 1024
SUBLANES = 8

IN_PROJ_ROWS = 512
IN_PROJ_COLS = 512
MIX_ROWS = 512
MIX_MATMUL_ROWS = 256
MIX_COLS = 512
CONV_HALO = 32
CONV_ROW_CHUNK = 128
CONV_LANE_CHUNK = 128
LN_ROW_CHUNK = 64
ATTN_UNROLL = 4
ATTN_BODY_STEPS = 4
FFN_ROWS = 512
FFN_COLS = 1024

LOG2E = float(np.log2(np.e))
Q_SCALE = HEAD_DIM ** -0.5 * LOG2E
ALIBI_PARTS = 3
V_PAD_ROWS = 16
MASKED = -1e30
NT_DIMS = (((1,), (1,)), ((), ()))


def _resident(shape):
    return pl.BlockSpec(shape, lambda *_: (0,) * len(shape), pipeline_mode=pl.Buffered(1))


def _alibi_slopes(n_heads):
    return (2.0 ** (-8.0 * np.arange(1, n_heads + 1) / n_heads)).astype(np.float32)


def _in_proj_kernel(x_ref, g_ref, wk_ref, wqvt_ref, wglu_ref, bglu_ref,
                    k_ref, qvt_ref, kmean_ref, h_ref, *, scale):
    rows = x_ref.shape[1]
    conv_ch = h_ref.shape[2]
    x = x_ref[0]
    ms = jnp.mean(x * x, axis=-1, keepdims=True)
    a = (x * lax.rsqrt(ms + RMS_EPS) * g_ref[...]).astype(BF16)

    for c in range(0, ATTN_WIDTH, IN_PROJ_COLS):
        r = jnp.dot(a, wk_ref[:, c:c + IN_PROJ_COLS], preferred_element_type=F32)
        blocks = r.reshape(rows // MOBA_BLOCK, MOBA_BLOCK, IN_PROJ_COLS)
        kmean_ref[0, 0, :, c:c + IN_PROJ_COLS] = jnp.mean(blocks, axis=1)
        k_ref[0, :, c:c + IN_PROJ_COLS] = r.astype(BF16)

    for rc in range(0, 2 * ATTN_WIDTH, MOBA_BLOCK):
        r = lax.dot_general(wqvt_ref[rc:rc + MOBA_BLOCK, :], a, NT_DIMS,
                            preferred_element_type=F32)
        if rc < ATTN_WIDTH:
            r = r * scale
        r = r.astype(BF16)
        for blk in range(rows // MOBA_BLOCK):
            qvt_ref[0, blk, rc:rc + MOBA_BLOCK, :] = r[:, blk * MOBA_BLOCK:(blk + 1) * MOBA_BLOCK]

    for c in range(0, conv_ch, IN_PROJ_COLS):
        val = jnp.dot(a, wglu_ref[:, c:c + IN_PROJ_COLS], preferred_element_type=F32)
        val = val + bglu_ref[:, c:c + IN_PROJ_COLS]
        gt = jnp.dot(a, wglu_ref[:, conv_ch + c:conv_ch + c + IN_PROJ_COLS], preferred_element_type=F32)
        gt = gt + bglu_ref[:, conv_ch + c:conv_ch + c + IN_PROJ_COLS]
        h_ref[0, :, c:c + IN_PROJ_COLS] = val * (1.0 / (1.0 + jnp.exp(-gt)))


def _in_proj(x, g, w_k, w_qvt, w_glu, b_glu):
    B, S, D = x.shape
    conv_ch = w_glu.shape[1] // 2
    rows = IN_PROJ_ROWS
    n_blk = rows // MOBA_BLOCK
    grid = (B, S // rows)
    out_shape = (
        jax.ShapeDtypeStruct((B, S, ATTN_WIDTH), BF16),
        jax.ShapeDtypeStruct((B, S // MOBA_BLOCK, 2 * ATTN_WIDTH, MOBA_BLOCK), BF16),
        jax.ShapeDtypeStruct((B, S // rows, n_blk, ATTN_WIDTH), F32),
        jax.ShapeDtypeStruct((B, S, conv_ch), F32),
    )
    return pl.pallas_call(
        functools.partial(_in_proj_kernel, scale=Q_SCALE),
        grid=grid,
        in_specs=[
            pl.BlockSpec((1, rows, D), lambda b, i: (b, i, 0)),
            _resident((1, D)),
            _resident(w_k.shape),
            _resident(w_qvt.shape),
            _resident(w_glu.shape),
            _resident((1, 2 * conv_ch)),
        ],
        out_specs=(
            pl.BlockSpec((1, rows, ATTN_WIDTH), lambda b, i: (b, i, 0)),
            pl.BlockSpec((1, n_blk, 2 * ATTN_WIDTH, MOBA_BLOCK), lambda b, i: (b, i, 0, 0)),
            pl.BlockSpec((1, 1, n_blk, ATTN_WIDTH), lambda b, i: (b, i, 0, 0)),
            pl.BlockSpec((1, rows, conv_ch), lambda b, i: (b, i, 0)),
        ),
        out_shape=out_shape,
        compiler_params=pltpu.CompilerParams(
            dimension_semantics=("arbitrary", "arbitrary"),
            vmem_limit_bytes=V7X_VMEM_LIMIT_BYTES),
        name="in_proj",
    )(x, g, w_k, w_qvt, w_glu, b_glu)


def _pair_schedule(n_blocks, unroll):
    todo = {qi: list(range(qi)) for qi in range(1, n_blocks)}
    key_blocks, query_tiles = [], []
    while any(todo.values()):
        tiles = sorted((qi for qi in todo if todo[qi]), key=lambda t: -len(todo[t]))[:unroll]
        for qi in tiles:
            key_blocks.append(todo[qi].pop(0))
            query_tiles.append(qi)
        for _ in range(unroll - len(tiles)):
            key_blocks.append(0)
            query_tiles.append(n_blocks)
    while (len(key_blocks) // unroll) % ATTN_BODY_STEPS:
        key_blocks += [0] * unroll
        query_tiles += [n_blocks] * unroll
    return np.asarray(key_blocks, np.int32), np.asarray(query_tiles, np.int32)


def _moba_kernel(slopes_ref, pair_k_ref, pair_q_ref, qext_ref, qt_ref, k_ref, vt_ref, kmean_ref, o_ref,
                 qaug_ref, kaug_ref, vaug_ref, pen_ref, m_ref, acc_ref,
                 s_even_ref, s_odd_ref, smax_even_ref, smax_odd_ref):
    L = MOBA_BLOCK
    U = ATTN_UNROLL
    Dh = HEAD_DIM
    n_blocks = vt_ref.shape[1]
    n_steps = pair_k_ref.shape[0] // U
    slope = slopes_ref[pl.program_id(1)]

    lane = lax.broadcasted_iota(jnp.int32, (L, Dh), 1)
    key_cols = jnp.where(lane < ALIBI_PARTS, lax.broadcasted_iota(jnp.int32, (L, Dh), 0), 0).astype(BF16)
    ones_row = (lax.broadcasted_iota(jnp.int32, (V_PAD_ROWS, L), 0) == 0).astype(BF16)
    for blk in range(n_blocks):
        kaug_ref[blk * L:(blk + 1) * L, 0:Dh] = k_ref[0, blk * L:(blk + 1) * L, :]
        kaug_ref[blk * L:(blk + 1) * L, Dh:2 * Dh] = key_cols
        qaug_ref[blk, 0:Dh, :] = qt_ref[0, blk]
        qaug_ref[blk, Dh:2 * Dh, :] = qext_ref[0]
        vaug_ref[blk, 0:Dh, :] = vt_ref[0, blk]
        vaug_ref[blk, Dh:Dh + V_PAD_ROWS, :] = ones_row

    key_local = lax.broadcasted_iota(jnp.int32, (L, L), 0)
    qry_local = lax.broadcasted_iota(jnp.int32, (L, L), 1)
    kmean = kmean_ref[0].astype(BF16)
    blk_id = lax.broadcasted_iota(jnp.int32, (n_blocks, L), 0).astype(F32)

    def init_tiles(g, carry):
        tiles = [g * U + u for u in range(U)]
        for u, qi in enumerate(tiles):
            k_own = kaug_ref[pl.ds(pl.multiple_of(qi * L, L), L), :]
            s_even_ref[u] = jnp.dot(k_own, qaug_ref[qi], preferred_element_type=F32)
        gates = [jnp.dot(kmean, qt_ref[0, qi], preferred_element_type=F32) for qi in tiles]
        for u, qi in enumerate(tiles):
            gate = jnp.where(blk_id < jnp.asarray(qi, F32), gates[u], -jnp.inf)
            sel = jnp.zeros((n_blocks, L), F32)
            for r in range(MOBA_TOPK):
                best = jnp.max(gate, axis=0, keepdims=True)
                first = jnp.min(jnp.where(gate == best, blk_id, float(n_blocks)), axis=0, keepdims=True)
                hit = blk_id == first
                sel = jnp.maximum(sel, jnp.where(hit, jnp.asarray(qi > r, F32), 0.0))
                gate = jnp.where(hit, -jnp.inf, gate)
            pen_ref[qi] = jnp.where(sel > 0.0, 0.0, MASKED)

            s = jnp.where(key_local <= qry_local, s_even_ref[u], -jnp.inf)
            m = jnp.max(s, axis=0, keepdims=True)
            p = jnp.exp2(s - m)
            m_ref[qi] = m
            acc_ref[qi] = jnp.dot(vaug_ref[qi], p.astype(BF16), preferred_element_type=F32)
        return carry

    lax.fori_loop(0, n_blocks // U, init_tiles, 0)
    pen_ref[n_blocks] = jnp.full((n_blocks, L), MASKED, F32)
    m_ref[n_blocks] = jnp.zeros((1, L), F32)
    acc_ref[n_blocks] = jnp.zeros(acc_ref.shape[1:], F32)

    def scores(step, s_ref, smax_ref):
        for u in range(U):
            j = pair_k_ref[step * U + u]
            qi = jnp.minimum(pair_q_ref[step * U + u], n_blocks - 1)
            k_blk = kaug_ref[pl.ds(pl.multiple_of(j * L, L), L), :]
            s = jnp.dot(k_blk, qaug_ref[qi], preferred_element_type=F32)
            s_ref[u] = s
            smax_ref[u] = jnp.max(s, axis=0, keepdims=True)

    def update(step, s_ref, smax_ref):
        loaded = []
        for u in range(U):
            j = pair_k_ref[step * U + u]
            slot = pair_q_ref[step * U + u]
            row = pen_ref[slot, pl.ds(j, 1), :] - slope * float(L) * jnp.asarray(slot - j, F32)
            loaded.append((u, j, slot, row, m_ref[slot]))
        updated = []
        for u, j, slot, row, m in loaded:
            m_new = jnp.maximum(m, smax_ref[u] + row)
            alpha = jnp.exp2(m - m_new)
            p = jnp.exp2(s_ref[u] + (row - m_new))
            pv = jnp.dot(vaug_ref[j], p.astype(BF16), preferred_element_type=F32)
            acc_ref[slot] = alpha * acc_ref[slot] + pv
            updated.append((slot, m_new))
        for slot, m in updated:
            m_ref[slot] = m

    def several_steps(t, carry):
        for i in range(ATTN_BODY_STEPS):
            step = ATTN_BODY_STEPS * t + i
            cur, nxt = (even, odd) if i % 2 == 0 else (odd, even)
            scores(jnp.minimum(step + 1, n_steps - 1), *nxt)
            update(step, *cur)
        return carry

    even, odd = (s_even_ref, smax_even_ref), (s_odd_ref, smax_odd_ref)
    scores(0, *even)
    lax.fori_loop(0, n_steps // ATTN_BODY_STEPS, several_steps, 0)

    def finish_tiles(g, carry):
        for u in range(U):
            qi = g * U + u
            q0 = pl.multiple_of(qi * L, L)
            acc = acc_ref[qi]
            o_ref[0, pl.ds(q0, L), :] = (acc[0:Dh] / acc[Dh:Dh + 1]).T.astype(o_ref.dtype)
        return carry

    lax.fori_loop(0, n_blocks // U, finish_tiles, 0)


def _bf16_parts(x, n):
    parts, rest = [], np.asarray(x, np.float32)
    for _ in range(n):
        part = rest.astype(BF16).astype(np.float32)
        parts.append(part)
        rest = (rest - part).astype(np.float32)
    return np.stack(parts, axis=-1)


def _moba(k, qvt, kmean):
    B, S, _ = k.shape
    n_blocks = S // MOBA_BLOCK
    assert n_blocks % ATTN_UNROLL == 0 and ATTN_BODY_STEPS % 2 == 0
    pair_k, pair_q = _pair_schedule(n_blocks, ATTN_UNROLL)
    slopes = (_alibi_slopes(N_HEADS) * LOG2E).astype(np.float32)
    qext = np.zeros((N_HEADS, HEAD_DIM, MOBA_BLOCK), np.float32)
    qext[:, :ALIBI_PARTS, :] = _bf16_parts(slopes, ALIBI_PARTS)[:, :, None]
    smem = pl.BlockSpec(memory_space=pltpu.SMEM)
    return pl.pallas_call(
        _moba_kernel,
        grid=(B, N_HEADS),
        in_specs=[
            smem, smem, smem,
            pl.BlockSpec((1, HEAD_DIM, MOBA_BLOCK), lambda b, h: (h, 0, 0)),
            pl.BlockSpec((1, n_blocks, HEAD_DIM, MOBA_BLOCK), lambda b, h: (b, 0, h, 0)),
            pl.BlockSpec((1, S, HEAD_DIM), lambda b, h: (b, 0, h)),
            pl.BlockSpec((1, n_blocks, HEAD_DIM, MOBA_BLOCK), lambda b, h: (b, 0, N_HEADS + h, 0)),
            pl.BlockSpec((1, n_blocks, HEAD_DIM), lambda b, h: (b, 0, h)),
        ],
        out_specs=pl.BlockSpec((1, S, HEAD_DIM), lambda b, h: (b, 0, h)),
        out_shape=jax.ShapeDtypeStruct((B, S, ATTN_WIDTH), BF16),
        scratch_shapes=[
            pltpu.VMEM((n_blocks, 2 * HEAD_DIM, MOBA_BLOCK), BF16),
            pltpu.VMEM((S, 2 * HEAD_DIM), BF16),
            pltpu.VMEM((n_blocks, HEAD_DIM + V_PAD_ROWS, MOBA_BLOCK), BF16),
            pltpu.VMEM((n_blocks + 1, n_blocks, MOBA_BLOCK), F32),
            pltpu.VMEM((n_blocks + 1, 1, MOBA_BLOCK), F32),
            pltpu.VMEM((n_blocks + 1, HEAD_DIM + V_PAD_ROWS, MOBA_BLOCK), F32),
            pltpu.VMEM((ATTN_UNROLL, MOBA_BLOCK, MOBA_BLOCK), F32),
            pltpu.VMEM((ATTN_UNROLL, MOBA_BLOCK, MOBA_BLOCK), F32),
            pltpu.VMEM((ATTN_UNROLL, 1, MOBA_BLOCK), F32),
            pltpu.VMEM((ATTN_UNROLL, 1, MOBA_BLOCK), F32),
        ],
        compiler_params=pltpu.CompilerParams(
            dimension_semantics=("arbitrary", "arbitrary"),
            vmem_limit_bytes=V7X_VMEM_LIMIT_BYTES),
        name="moba",
    )(jnp.asarray(slopes), jnp.asarray(pair_k), jnp.asarray(pair_q), jnp.asarray(qext, BF16),
      qvt, k, qvt, kmean)


def _mix_kernel(x_ref, attn_ref, hmain_ref, hhalo_ref, wdw_ref, bdw_ref, lng_ref, lnb_ref,
                wo_ref, gpost_ref, gpre_ref, h1_ref, f_ref, ext_ref, y_ref, conv_ref, mixed_ref):
    rows = x_ref.shape[1]
    conv_ch = hmain_ref.shape[2]
    first_tile = pl.program_id(1) == 0

    ext_ref[0:CONV_HALO, :] = jnp.where(first_tile, 0.0, hhalo_ref[0])
    ext_ref[CONV_HALO:CONV_HALO + rows, :] = hmain_ref[0]

    attn = attn_ref[0]
    attn_w = attn.shape[1]
    d_model = wo_ref.shape[1]
    for c in range(0, d_model, MIX_COLS):
        mixed_ref[:, c:c + MIX_COLS] = jnp.dot(attn, wo_ref[0:attn_w, c:c + MIX_COLS],
                                               preferred_element_type=F32)

    shift0 = CONV_HALO - (CONV_WIDTH - 1)
    R, CL = CONV_ROW_CHUNK, CONV_LANE_CHUNK
    for r0 in range(0, rows, MIX_MATMUL_ROWS):
        for rc in range(r0, r0 + MIX_MATMUL_ROWS, R):
            for c0 in range(0, conv_ch, CL):
                y = None
                for b in range(SUBLANES):
                    z_rows = R if b == 0 else R + SUBLANES
                    z = None
                    for a in range((shift0 + CONV_WIDTH - 1) // SUBLANES + 1):
                        w = SUBLANES * a + b - shift0
                        if 0 <= w < CONV_WIDTH:
                            lo = rc + SUBLANES * a
                            tap = ext_ref[lo:lo + z_rows, c0:c0 + CL] * wdw_ref[w:w + 1, c0:c0 + CL]
                            z = tap if z is None else z + tap
                    part = z[b:b + R]
                    y = part if y is None else y + part
                y_ref[rc:rc + R, c0:c0 + CL] = y + bdw_ref[:, c0:c0 + CL]

        for rl in range(r0, r0 + MIX_MATMUL_ROWS, LN_ROW_CHUNK):
            y = y_ref[rl:rl + LN_ROW_CHUNK, :]
            mu = jnp.mean(y, axis=-1, keepdims=True)
            var = jnp.mean(jnp.square(y - mu), axis=-1, keepdims=True)
            y = (y - mu) * lax.rsqrt(var + LN_EPS) * lng_ref[...] + lnb_ref[...]
            y = y * (1.0 / (1.0 + jnp.exp(-y)))
            conv_ref[rl:rl + LN_ROW_CHUNK, :] = y.astype(BF16)

        conv = conv_ref[r0:r0 + MIX_MATMUL_ROWS, :]
        cols = []
        for c in range(0, d_model, MIX_COLS):
            r = jnp.dot(conv, wo_ref[attn_w:attn_w + conv_ch, c:c + MIX_COLS], preferred_element_type=F32)
            cols.append(mixed_ref[r0:r0 + MIX_MATMUL_ROWS, c:c + MIX_COLS] + r)
        mixed = jnp.concatenate(cols, axis=-1)
        ms = jnp.mean(mixed * mixed, axis=-1, keepdims=True)
        h1 = x_ref[0, r0:r0 + MIX_MATMUL_ROWS, :] + mixed * lax.rsqrt(ms + RMS_EPS) * gpost_ref[...]
        h1_ref[0, r0:r0 + MIX_MATMUL_ROWS, :] = h1
        ms1 = jnp.mean(h1 * h1, axis=-1, keepdims=True)
        f_ref[0, r0:r0 + MIX_MATMUL_ROWS, :] = (h1 * lax.rsqrt(ms1 + RMS_EPS) * gpre_ref[...]).astype(BF16)


def _mix(x, attn, hglu, w_dw, b_dw, ln_g, ln_b, w_out, g_post, g_pre):
    B, S, D = x.shape
    conv_ch = hglu.shape[2]
    rows = MIX_ROWS
    halo_per_tile = rows // CONV_HALO
    return pl.pallas_call(
        _mix_kernel,
        grid=(B, S // rows),
        in_specs=[
            pl.BlockSpec((1, rows, D), lambda b, i: (b, i, 0)),
            pl.BlockSpec((1, rows, attn.shape[2]), lambda b, i: (b, i, 0)),
            pl.BlockSpec((1, rows, conv_ch), lambda b, i: (b, i, 0)),
            pl.BlockSpec((1, CONV_HALO, conv_ch),
                         lambda b, i: (b, jnp.maximum(i * halo_per_tile - 1, 0), 0)),
            _resident(w_dw.shape),
            _resident((1, conv_ch)),
            _resident((1, conv_ch)),
            _resident((1, conv_ch)),
            _resident(w_out.shape),
            _resident((1, D)),
            _resident((1, D)),
        ],
        out_specs=(
            pl.BlockSpec((1, rows, D), lambda b, i: (b, i, 0)),
            pl.BlockSpec((1, rows, D), lambda b, i: (b, i, 0)),
        ),
        out_shape=(jax.ShapeDtypeStruct((B, S, D), F32), jax.ShapeDtypeStruct((B, S, D), BF16)),
        scratch_shapes=[
            pltpu.VMEM((CONV_HALO + rows, conv_ch), F32),
            pltpu.VMEM((rows, conv_ch), F32),
            pltpu.VMEM((rows, conv_ch), BF16),
            pltpu.VMEM((rows, D), F32),
        ],
        compiler_params=pltpu.CompilerParams(
            dimension_semantics=("arbitrary", "arbitrary"),
            vmem_limit_bytes=V7X_VMEM_LIMIT_BYTES),
        name="mix",
    )(x, attn, hglu, hglu, w_dw, b_dw, ln_g, ln_b, w_out, g_post, g_pre)


def _ffn_kernel(f_ref, h1_ref, w1_ref, w2_ref, g_ref, o_ref, acc_ref):
    kf = pl.program_id(2)

    @pl.when(kf == 0)
    def _():
        acc_ref[...] = jnp.zeros_like(acc_ref)

    u = jnp.dot(f_ref[0], w1_ref[...], preferred_element_type=F32)
    u = jnp.square(jnp.maximum(u, 0.0)).astype(BF16)
    acc_ref[...] += jnp.dot(u, w2_ref[...], preferred_element_type=F32)

    @pl.when(kf == pl.num_programs(2) - 1)
    def _():
        a = acc_ref[...]
        ms = jnp.mean(a * a, axis=-1, keepdims=True)
        o_ref[0] = h1_ref[0] + a * lax.rsqrt(ms + RMS_EPS) * g_ref[...]


def _ffn(f, h1, w1, w2, g):
    B, S, D = h1.shape
    d_ff = w1.shape[1]
    rows, cols = FFN_ROWS, FFN_COLS
    return pl.pallas_call(
        _ffn_kernel,
        grid=(B, S // rows, d_ff // cols),
        in_specs=[
            pl.BlockSpec((1, rows, D), lambda b, i, k: (b, i, 0)),
            pl.BlockSpec((1, rows, D), lambda b, i, k: (b, i, 0)),
            pl.BlockSpec((D, cols), lambda b, i, k: (0, k)),
            pl.BlockSpec((cols, D), lambda b, i, k: (k, 0)),
            _resident((1, D)),
        ],
        out_specs=pl.BlockSpec((1, rows, D), lambda b, i, k: (b, i, 0)),
        out_shape=jax.ShapeDtypeStruct((B, S, D), F32),
        scratch_shapes=[pltpu.VMEM((rows, D), F32)],
        compiler_params=pltpu.CompilerParams(
            dimension_semantics=("arbitrary", "arbitrary", "arbitrary"),
            vmem_limit_bytes=V7X_VMEM_LIMIT_BYTES),
        name="ffn",
    )(f, h1, w1, w2, g)


def kernel(x, g_mix_pre, w_in, b_glu, w_dw, b_dw, ln_conv_g, ln_conv_b, w_out, g_mix_post,
           g_ffn_pre, w_ff1, w_ff2, g_ffn_post):
    B, S, D = x.shape
    depth = w_in.shape[0]
    assert S % IN_PROJ_ROWS == 0 and S % MIX_ROWS == 0 and S % FFN_ROWS == 0
    assert IN_PROJ_ROWS % MOBA_BLOCK == 0 and MIX_ROWS % CONV_HALO == 0
    assert MIX_ROWS % MIX_MATMUL_ROWS == 0 and MIX_MATMUL_ROWS % CONV_ROW_CHUNK == 0
    assert CONV_HALO >= CONV_WIDTH - 1 and CONV_HALO % SUBLANES == 0
    row = lambda v: v.reshape(1, -1)

    h = x
    for l in range(depth):
        w_q = w_in[l, :, :ATTN_WIDTH]
        w_k = w_in[l, :, ATTN_WIDTH:2 * ATTN_WIDTH].astype(BF16)
        w_v = w_in[l, :, 2 * ATTN_WIDTH:3 * ATTN_WIDTH]
        w_qvt = jnp.concatenate([w_q, w_v], axis=1).T.astype(BF16)
        w_glu = w_in[l, :, 3 * ATTN_WIDTH:].astype(BF16)

        k, qvt, kmean, hglu = _in_proj(h, row(g_mix_pre[l]), w_k, w_qvt, w_glu, row(b_glu[l]))
        kmean = kmean.reshape(B, S // MOBA_BLOCK, ATTN_WIDTH)
        attn = _moba(k, qvt, kmean)
        h1, f = _mix(h, attn, hglu, w_dw[l], row(b_dw[l]), row(ln_conv_g[l]), row(ln_conv_b[l]),
                     w_out[l].astype(BF16), row(g_mix_post[l]), row(g_ffn_pre[l]))
        h = _ffn(f, h1, w_ff1[l].astype(BF16), w_ff2[l].astype(BF16), row(g_ffn_post[l]))
    return h
```

```python
import functools

import numpy as np
import jax
import jax.numpy as jnp
from jax import lax
from jax.experimental import pallas as pl
from jax.experimental.pallas import tpu as pltpu

F32 = jnp.float32
BF16 = jnp.bfloat16

HEAD_DIM = 128
N_HEADS = 8
ATTN_WIDTH = N_HEADS * HEAD_DIM
MOBA_BLOCK = 256
MOBA_TOPK = 3
CONV_WIDTH = 31
RMS_EPS = 1e-6
LN_EPS = 1e-5

V7X_VMEM_LIMIT_BYTES = 60000 * 1024
SUBLANES = 8

IN_PROJ_ROWS = 512
IN_PROJ_COLS = 512
MIX_ROWS = 512
MIX_MATMUL_ROWS = 256
MIX_COLS = 512
CONV_HALO = 32
CONV_ROW_CHUNK = 128
CONV_LANE_CHUNK = 128
LN_ROW_CHUNK = 64
ATTN_UNROLL = 4
ATTN_BODY_STEPS = 4
FFN_ROWS = 512
FFN_COLS = 1024
FFN_CAST_COLS = 512

LOG2E = float(np.log2(np.e))
Q_SCALE = HEAD_DIM ** -0.5 * LOG2E
ALIBI_PARTS = 3
V_PAD_ROWS = 16
MASKED = -1e30
NT_DIMS = (((1,), (1,)), ((), ()))


def _resident(shape):
    return pl.BlockSpec(shape, lambda *_: (0,) * len(shape), pipeline_mode=pl.Buffered(1))


def _alibi_slopes(n_heads):
    return (2.0 ** (-8.0 * np.arange(1, n_heads + 1) / n_heads)).astype(np.float32)


def _in_proj_kernel(x_ref, g_ref, wk_ref, wqvt_ref, wglu_ref, bglu_ref,
                    k_ref, qvt_ref, kmean_ref, h_ref, *, scale):
    rows = x_ref.shape[1]
    conv_ch = h_ref.shape[2]
    x = x_ref[0]
    ms = jnp.mean(x * x, axis=-1, keepdims=True)
    a = (x * lax.rsqrt(ms + RMS_EPS) * g_ref[...]).astype(BF16)

    for c in range(0, ATTN_WIDTH, IN_PROJ_COLS):
        r = jnp.dot(a, wk_ref[:, c:c + IN_PROJ_COLS], preferred_element_type=F32)
        blocks = r.reshape(rows // MOBA_BLOCK, MOBA_BLOCK, IN_PROJ_COLS)
        kmean_ref[0, 0, :, c:c + IN_PROJ_COLS] = jnp.mean(blocks, axis=1)
        k_ref[0, :, c:c + IN_PROJ_COLS] = r.astype(BF16)

    for rc in range(0, 2 * ATTN_WIDTH, MOBA_BLOCK):
        r = lax.dot_general(wqvt_ref[rc:rc + MOBA_BLOCK, :], a, NT_DIMS,
                            preferred_element_type=F32)
        if rc < ATTN_WIDTH:
            r = r * scale
        r = r.astype(BF16)
        for blk in range(rows // MOBA_BLOCK):
            qvt_ref[0, blk, rc:rc + MOBA_BLOCK, :] = r[:, blk * MOBA_BLOCK:(blk + 1) * MOBA_BLOCK]

    for c in range(0, conv_ch, IN_PROJ_COLS):
        val = jnp.dot(a, wglu_ref[:, c:c + IN_PROJ_COLS], preferred_element_type=F32)
        val = val + bglu_ref[:, c:c + IN_PROJ_COLS]
        gt = jnp.dot(a, wglu_ref[:, conv_ch + c:conv_ch + c + IN_PROJ_COLS], preferred_element_type=F32)
        gt = gt + bglu_ref[:, conv_ch + c:conv_ch + c + IN_PROJ_COLS]
        h_ref[0, :, c:c + IN_PROJ_COLS] = val * (1.0 / (1.0 + jnp.exp(-gt)))


def _in_proj(x, g, w_k, w_qvt, w_glu, b_glu):
    B, S, D = x.shape
    conv_ch = w_glu.shape[1] // 2
    rows = IN_PROJ_ROWS
    n_blk = rows // MOBA_BLOCK
    grid = (B, S // rows)
    out_shape = (
        jax.ShapeDtypeStruct((B, S, ATTN_WIDTH), BF16),
        jax.ShapeDtypeStruct((B, S // MOBA_BLOCK, 2 * ATTN_WIDTH, MOBA_BLOCK), BF16),
        jax.ShapeDtypeStruct((B, S // rows, n_blk, ATTN_WIDTH), F32),
        jax.ShapeDtypeStruct((B, S, conv_ch), F32),
    )
    return pl.pallas_call(
        functools.partial(_in_proj_kernel, scale=Q_SCALE),
        grid=grid,
        in_specs=[
            pl.BlockSpec((1, rows, D), lambda b, i: (b, i, 0)),
            _resident((1, D)),
            _resident(w_k.shape),
            _resident(w_qvt.shape),
            _resident(w_glu.shape),
            _resident((1, 2 * conv_ch)),
        ],
        out_specs=(
            pl.BlockSpec((1, rows, ATTN_WIDTH), lambda b, i: (b, i, 0)),
            pl.BlockSpec((1, n_blk, 2 * ATTN_WIDTH, MOBA_BLOCK), lambda b, i: (b, i, 0, 0)),
            pl.BlockSpec((1, 1, n_blk, ATTN_WIDTH), lambda b, i: (b, i, 0, 0)),
            pl.BlockSpec((1, rows, conv_ch), lambda b, i: (b, i, 0)),
        ),
        out_shape=out_shape,
        compiler_params=pltpu.CompilerParams(
            dimension_semantics=("arbitrary", "arbitrary"),
            vmem_limit_bytes=V7X_VMEM_LIMIT_BYTES),
        name="in_proj",
    )(x, g, w_k, w_qvt, w_glu, b_glu)


def _pair_schedule(n_blocks, unroll):
    todo = {qi: list(range(qi)) for qi in range(1, n_blocks)}
    key_blocks, query_tiles = [], []
    while any(todo.values()):
        tiles = sorted((qi for qi in todo if todo[qi]), key=lambda t: -len(todo[t]))[:unroll]
        for qi in tiles:
            key_blocks.append(todo[qi].pop(0))
            query_tiles.append(qi)
        for _ in range(unroll - len(tiles)):
            key_blocks.append(0)
            query_tiles.append(n_blocks)
    while (len(key_blocks) // unroll) % ATTN_BODY_STEPS:
        key_blocks += [0] * unroll
        query_tiles += [n_blocks] * unroll
    return np.asarray(key_blocks, np.int32), np.asarray(query_tiles, np.int32)


def _moba_kernel(slopes_ref, pair_k_ref, pair_q_ref, qext_ref, qt_ref, k_ref, vt_ref, kmean_ref, o_ref,
                 qaug_ref, kaug_ref, vaug_ref, pen_ref, m_ref, acc_ref,
                 s_even_ref, s_odd_ref, smax_even_ref, smax_odd_ref):
    L = MOBA_BLOCK
    U = ATTN_UNROLL
    Dh = HEAD_DIM
    n_blocks = vt_ref.shape[1]
    n_steps = pair_k_ref.shape[0] // U
    slope = slopes_ref[pl.program_id(1)]

    lane = lax.broadcasted_iota(jnp.int32, (L, Dh), 1)
    key_cols = jnp.where(lane < ALIBI_PARTS, lax.broadcasted_iota(jnp.int32, (L, Dh), 0), 0).astype(BF16)
    ones_row = (lax.broadcasted_iota(jnp.int32, (V_PAD_ROWS, L), 0) == 0).astype(BF16)
    for blk in range(n_blocks):
        kaug_ref[blk * L:(blk + 1) * L, 0:Dh] = k_ref[0, blk * L:(blk + 1) * L, :]
        kaug_ref[blk * L:(blk + 1) * L, Dh:2 * Dh] = key_cols
        qaug_ref[blk, 0:Dh, :] = qt_ref[0, blk]
        qaug_ref[blk, Dh:2 * Dh, :] = qext_ref[0]
        vaug_ref[blk, 0:Dh, :] = vt_ref[0, blk]
        vaug_ref[blk, Dh:Dh + V_PAD_ROWS, :] = ones_row

    key_local = lax.broadcasted_iota(jnp.int32, (L, L), 0)
    qry_local = lax.broadcasted_iota(jnp.int32, (L, L), 1)
    kmean = kmean_ref[0].astype(BF16)
    blk_id = lax.broadcasted_iota(jnp.int32, (n_blocks, L), 0).astype(F32)

    def init_tiles(g, carry):
        tiles = [g * U + u for u in range(U)]
        for u, qi in enumerate(tiles):
            k_own = kaug_ref[pl.ds(pl.multiple_of(qi * L, L), L), :]
            s_even_ref[u] = jnp.dot(k_own, qaug_ref[qi], preferred_element_type=F32)
        gates = [jnp.dot(kmean, qt_ref[0, qi], preferred_element_type=F32) for qi in tiles]
        for u, qi in enumerate(tiles):
            gate = jnp.where(blk_id < jnp.asarray(qi, F32), gates[u], -jnp.inf)
            sel = jnp.zeros((n_blocks, L), F32)
            for r in range(MOBA_TOPK):
                best = jnp.max(gate, axis=0, keepdims=True)
                first = jnp.min(jnp.where(gate == best, blk_id, float(n_blocks)), axis=0, keepdims=True)
                hit = blk_id == first
                sel = jnp.maximum(sel, jnp.where(hit, jnp.asarray(qi > r, F32), 0.0))
                gate = jnp.where(hit, -jnp.inf, gate)
            pen_ref[qi] = jnp.where(sel > 0.0, 0.0, MASKED)

            s = jnp.where(key_local <= qry_local, s_even_ref[u], -jnp.inf)
            m = jnp.max(s, axis=0, keepdims=True)
            p = jnp.exp2(s - m)
            m_ref[qi] = m
            acc_ref[qi] = jnp.dot(vaug_ref[qi], p.astype(BF16), preferred_element_type=F32)
        return carry

    lax.fori_loop(0, n_blocks // U, init_tiles, 0)
    pen_ref[n_blocks] = jnp.full((n_blocks, L), MASKED, F32)
    m_ref[n_blocks] = jnp.zeros((1, L), F32)
    acc_ref[n_blocks] = jnp.zeros(acc_ref.shape[1:], F32)

    def scores(step, s_ref, smax_ref):
        for u in range(U):
            j = pair_k_ref[step * U + u]
            qi = jnp.minimum(pair_q_ref[step * U + u], n_blocks - 1)
            k_blk = kaug_ref[pl.ds(pl.multiple_of(j * L, L), L), :]
            s = jnp.dot(k_blk, qaug_ref[qi], preferred_element_type=F32)
            s_ref[u] = s
            smax_ref[u] = jnp.max(s, axis=0, keepdims=True)

    def update(step, s_ref, smax_ref):
        loaded = []
        for u in range(U):
            j = pair_k_ref[step * U + u]
            slot = pair_q_ref[step * U + u]
            row = pen_ref[slot, pl.ds(j, 1), :] - slope * float(L) * jnp.asarray(slot - j, F32)
            loaded.append((u, j, slot, row, m_ref[slot]))
        updated = []
        for u, j, slot, row, m in loaded:
            m_new = jnp.maximum(m, smax_ref[u] + row)
            alpha = jnp.exp2(m - m_new)
            p = jnp.exp2(s_ref[u] + (row - m_new))
            pv = jnp.dot(vaug_ref[j], p.astype(BF16), preferred_element_type=F32)
            acc_ref[slot] = alpha * acc_ref[slot] + pv
            updated.append((slot, m_new))
        for slot, m in updated:
            m_ref[slot] = m

    def several_steps(t, carry):
        for i in range(ATTN_BODY_STEPS):
            step = ATTN_BODY_STEPS * t + i
            cur, nxt = (even, odd) if i % 2 == 0 else (odd, even)
            scores(jnp.minimum(step + 1, n_steps - 1), *nxt)
            update(step, *cur)
        return carry

    even, odd = (s_even_ref, smax_even_ref), (s_odd_ref, smax_odd_ref)
    scores(0, *even)
    lax.fori_loop(0, n_steps // ATTN_BODY_STEPS, several_steps, 0)

    def finish_tiles(g, carry):
        for u in range(U):
            qi = g * U + u
            q0 = pl.multiple_of(qi * L, L)
            acc = acc_ref[qi]
            o_ref[0, pl.ds(q0, L), :] = (acc[0:Dh] / acc[Dh:Dh + 1]).T.astype(o_ref.dtype)
        return carry

    lax.fori_loop(0, n_blocks // U, finish_tiles, 0)


def _bf16_parts(x, n):
    parts, rest = [], np.asarray(x, np.float32)
    for _ in range(n):
        part = rest.astype(BF16).astype(np.float32)
        parts.append(part)
        rest = (rest - part).astype(np.float32)
    return np.stack(parts, axis=-1)


def _moba(k, qvt, kmean):
    B, S, _ = k.shape
    n_blocks = S // MOBA_BLOCK
    assert n_blocks % ATTN_UNROLL == 0 and ATTN_BODY_STEPS % 2 == 0
    pair_k, pair_q = _pair_schedule(n_blocks, ATTN_UNROLL)
    slopes = (_alibi_slopes(N_HEADS) * LOG2E).astype(np.float32)
    qext = np.zeros((N_HEADS, HEAD_DIM, MOBA_BLOCK), np.float32)
    qext[:, :ALIBI_PARTS, :] = _bf16_parts(slopes, ALIBI_PARTS)[:, :, None]
    smem = pl.BlockSpec(memory_space=pltpu.SMEM)
    return pl.pallas_call(
        _moba_kernel,
        grid=(B, N_HEADS),
        in_specs=[
            smem, smem, smem,
            pl.BlockSpec((1, HEAD_DIM, MOBA_BLOCK), lambda b, h: (h, 0, 0)),
            pl.BlockSpec((1, n_blocks, HEAD_DIM, MOBA_BLOCK), lambda b, h: (b, 0, h, 0)),
            pl.BlockSpec((1, S, HEAD_DIM), lambda b, h: (b, 0, h)),
            pl.BlockSpec((1, n_blocks, HEAD_DIM, MOBA_BLOCK), lambda b, h: (b, 0, N_HEADS + h, 0)),
            pl.BlockSpec((1, n_blocks, HEAD_DIM), lambda b, h: (b, 0, h)),
        ],
        out_specs=pl.BlockSpec((1, S, HEAD_DIM), lambda b, h: (b, 0, h)),
        out_shape=jax.ShapeDtypeStruct((B, S, ATTN_WIDTH), BF16),
        scratch_shapes=[
            pltpu.VMEM((n_blocks, 2 * HEAD_DIM, MOBA_BLOCK), BF16),
            pltpu.VMEM((S, 2 * HEAD_DIM), BF16),
            pltpu.VMEM((n_blocks, HEAD_DIM + V_PAD_ROWS, MOBA_BLOCK), BF16),
            pltpu.VMEM((n_blocks + 1, n_blocks, MOBA_BLOCK), F32),
            pltpu.VMEM((n_blocks + 1, 1, MOBA_BLOCK), F32),
            pltpu.VMEM((n_blocks + 1, HEAD_DIM + V_PAD_ROWS, MOBA_BLOCK), F32),
            pltpu.VMEM((ATTN_UNROLL, MOBA_BLOCK, MOBA_BLOCK), F32),
            pltpu.VMEM((ATTN_UNROLL, MOBA_BLOCK, MOBA_BLOCK), F32),
            pltpu.VMEM((ATTN_UNROLL, 1, MOBA_BLOCK), F32),
            pltpu.VMEM((ATTN_UNROLL, 1, MOBA_BLOCK), F32),
        ],
        compiler_params=pltpu.CompilerParams(
            dimension_semantics=("arbitrary", "arbitrary"),
            vmem_limit_bytes=V7X_VMEM_LIMIT_BYTES),
        name="moba",
    )(jnp.asarray(slopes), jnp.asarray(pair_k), jnp.asarray(pair_q), jnp.asarray(qext, BF16),
      qvt, k, qvt, kmean)


def _mix_kernel(x_ref, attn_ref, hmain_ref, hhalo_ref, wdw_ref, bdw_ref, lng_ref, lnb_ref,
                wo_ref, gpost_ref, gpre_ref, h1_ref, f_ref, ext_ref, y_ref, conv_ref, mixed_ref):
    rows = x_ref.shape[1]
    conv_ch = hmain_ref.shape[2]
    first_tile = pl.program_id(1) == 0

    ext_ref[0:CONV_HALO, :] = jnp.where(first_tile, 0.0, hhalo_ref[0])
    ext_ref[CONV_HALO:CONV_HALO + rows, :] = hmain_ref[0]

    attn = attn_ref[0]
    attn_w = attn.shape[1]
    d_model = wo_ref.shape[1]
    for c in range(0, d_model, MIX_COLS):
        mixed_ref[:, c:c + MIX_COLS] = jnp.dot(attn, wo_ref[0:attn_w, c:c + MIX_COLS],
                                               preferred_element_type=F32)

    shift0 = CONV_HALO - (CONV_WIDTH - 1)
    R, CL = CONV_ROW_CHUNK, CONV_LANE_CHUNK
    for r0 in range(0, rows, MIX_MATMUL_ROWS):
        for rc in range(r0, r0 + MIX_MATMUL_ROWS, R):
            for c0 in range(0, conv_ch, CL):
                y = None
                for b in range(SUBLANES):
                    z_rows = R if b == 0 else R + SUBLANES
                    z = None
                    for a in range((shift0 + CONV_WIDTH - 1) // SUBLANES + 1):
                        w = SUBLANES * a + b - shift0
                        if 0 <= w < CONV_WIDTH:
                            lo = rc + SUBLANES * a
                            tap = ext_ref[lo:lo + z_rows, c0:c0 + CL] * wdw_ref[w:w + 1, c0:c0 + CL]
                            z = tap if z is None else z + tap
                    part = z[b:b + R]
                    y = part if y is None else y + part
                y_ref[rc:rc + R, c0:c0 + CL] = y + bdw_ref[:, c0:c0 + CL]

        for rl in range(r0, r0 + MIX_MATMUL_ROWS, LN_ROW_CHUNK):
            y = y_ref[rl:rl + LN_ROW_CHUNK, :]
            mu = jnp.mean(y, axis=-1, keepdims=True)
            var = jnp.mean(jnp.square(y - mu), axis=-1, keepdims=True)
            y = (y - mu) * lax.rsqrt(var + LN_EPS) * lng_ref[...] + lnb_ref[...]
            y = y * (1.0 / (1.0 + jnp.exp(-y)))
            conv_ref[rl:rl + LN_ROW_CHUNK, :] = y.astype(BF16)

        conv = conv_ref[r0:r0 + MIX_MATMUL_ROWS, :]
        cols = []
        for c in range(0, d_model, MIX_COLS):
            r = jnp.dot(conv, wo_ref[attn_w:attn_w + conv_ch, c:c + MIX_COLS], preferred_element_type=F32)
            cols.append(mixed_ref[r0:r0 + MIX_MATMUL_ROWS, c:c + MIX_COLS] + r)
        mixed = jnp.concatenate(cols, axis=-1)
        ms = jnp.mean(mixed * mixed, axis=-1, keepdims=True)
        h1 = x_ref[0, r0:r0 + MIX_MATMUL_ROWS, :] + mixed * lax.rsqrt(ms + RMS_EPS) * gpost_ref[...]
        h1_ref[0, r0:r0 + MIX_MATMUL_ROWS, :] = h1
        ms1 = jnp.mean(h1 * h1, axis=-1, keepdims=True)
        f_ref[0, r0:r0 + MIX_MATMUL_ROWS, :] = (h1 * lax.rsqrt(ms1 + RMS_EPS) * gpre_ref[...]).astype(BF16)


def _mix(x, attn, hglu, w_dw, b_dw, ln_g, ln_b, w_out, g_post, g_pre):
    B, S, D = x.shape
    conv_ch = hglu.shape[2]
    rows = MIX_ROWS
    halo_per_tile = rows // CONV_HALO
    return pl.pallas_call(
        _mix_kernel,
        grid=(B, S // rows),
        in_specs=[
            pl.BlockSpec((1, rows, D), lambda b, i: (b, i, 0)),
            pl.BlockSpec((1, rows, attn.shape[2]), lambda b, i: (b, i, 0)),
            pl.BlockSpec((1, rows, conv_ch), lambda b, i: (b, i, 0)),
            pl.BlockSpec((1, CONV_HALO, conv_ch),
                         lambda b, i: (b, jnp.maximum(i * halo_per_tile - 1, 0), 0)),
            _resident(w_dw.shape),
            _resident((1, conv_ch)),
            _resident((1, conv_ch)),
            _resident((1, conv_ch)),
            _resident(w_out.shape),
            _resident((1, D)),
            _resident((1, D)),
        ],
        out_specs=(
            pl.BlockSpec((1, rows, D), lambda b, i: (b, i, 0)),
            pl.BlockSpec((1, rows, D), lambda b, i: (b, i, 0)),
        ),
        out_shape=(jax.ShapeDtypeStruct((B, S, D), F32), jax.ShapeDtypeStruct((B, S, D), BF16)),
        scratch_shapes=[
            pltpu.VMEM((CONV_HALO + rows, conv_ch), F32),
            pltpu.VMEM((rows, conv_ch), F32),
            pltpu.VMEM((rows, conv_ch), BF16),
            pltpu.VMEM((rows, D), F32),
        ],
        compiler_params=pltpu.CompilerParams(
            dimension_semantics=("arbitrary", "arbitrary"),
            vmem_limit_bytes=V7X_VMEM_LIMIT_BYTES),
        name="mix",
    )(x, attn, hglu, hglu, w_dw, b_dw, ln_g, ln_b, w_out, g_post, g_pre)


def _ffn_chunk(f, w1, w2, acc_ref):
    u = jnp.dot(f, w1, preferred_element_type=F32)
    u = jnp.square(jnp.maximum(u, 0.0)).astype(BF16)
    acc_ref[...] += jnp.dot(u, w2, preferred_element_type=F32)


def _ffn_finish(h1_ref, g_ref, acc_ref, o_ref):
    a = acc_ref[...]
    ms = jnp.mean(a * a, axis=-1, keepdims=True)
    o_ref[0] = h1_ref[0] + a * lax.rsqrt(ms + RMS_EPS) * g_ref[...]


def _ffn_first_kernel(f_ref, h1_ref, w1_ref, w2_ref, g_ref, o_ref, w1b_ref, w2b_ref, acc_ref):
    kf = pl.program_id(0)

    @pl.when(kf == 0)
    def _():
        acc_ref[...] = jnp.zeros_like(acc_ref)

    w1 = w1_ref[...].astype(BF16)
    w2 = w2_ref[...].astype(BF16)
    w1b_ref[...] = w1
    w2b_ref[...] = w2
    _ffn_chunk(f_ref[0], w1, w2, acc_ref)

    @pl.when(kf == pl.num_programs(0) - 1)
    def _():
        _ffn_finish(h1_ref, g_ref, acc_ref, o_ref)


def _ffn_rest_kernel(f_ref, h1_ref, w1_ref, w2_ref, g_ref, first_ref, o_ref, acc_ref):
    tile, kf = pl.program_id(0), pl.program_id(1)
    last = kf == pl.num_programs(1) - 1
    computed = tile > 0

    @pl.when(jnp.logical_and(jnp.logical_not(computed), last))
    def _():
        o_ref[...] = first_ref[...]

    @pl.when(jnp.logical_and(computed, kf == 0))
    def _():
        acc_ref[...] = jnp.zeros_like(acc_ref)

    @pl.when(computed)
    def _():
        _ffn_chunk(f_ref[0], w1_ref[...], w2_ref[...], acc_ref)

    @pl.when(jnp.logical_and(computed, last))
    def _():
        _ffn_finish(h1_ref, g_ref, acc_ref, o_ref)


def _ffn(f, h1, w1, w2, g):
    B, S, D = h1.shape
    d_ff = w1.shape[1]
    rows, cols, cast_cols = FFN_ROWS, FFN_COLS, FFN_CAST_COLS
    tiles_per_seq = S // rows
    first, w1b, w2b = pl.pallas_call(
        _ffn_first_kernel,
        grid=(d_ff // cast_cols,),
        in_specs=[
            pl.BlockSpec((1, rows, D), lambda k: (0, 0, 0)),
            pl.BlockSpec((1, rows, D), lambda k: (0, 0, 0)),
            pl.BlockSpec((D, cast_cols), lambda k: (0, k)),
            pl.BlockSpec((cast_cols, D), lambda k: (k, 0)),
            _resident((1, D)),
        ],
        out_specs=(
            pl.BlockSpec((1, rows, D), lambda k: (0, 0, 0)),
            pl.BlockSpec((D, cast_cols), lambda k: (0, k)),
            pl.BlockSpec((cast_cols, D), lambda k: (k, 0)),
        ),
        out_shape=(jax.ShapeDtypeStruct((1, rows, D), F32),
                   jax.ShapeDtypeStruct(w1.shape, BF16), jax.ShapeDtypeStruct(w2.shape, BF16)),
        scratch_shapes=[pltpu.VMEM((rows, D), F32)],
        compiler_params=pltpu.CompilerParams(
            dimension_semantics=("arbitrary",), vmem_limit_bytes=V7X_VMEM_LIMIT_BYTES),
        name="ffn_first",
    )(f, h1, w1, w2, g)

    row_block = lambda t, k: (t // tiles_per_seq, t % tiles_per_seq, 0)
    chunk = lambda t, k: jnp.where(t == 0, 0, k)
    return pl.pallas_call(
        _ffn_rest_kernel,
        grid=(B * tiles_per_seq, d_ff // cols),
        in_specs=[
            pl.BlockSpec((1, rows, D), row_block),
            pl.BlockSpec((1, rows, D), row_block),
            pl.BlockSpec((D, cols), lambda t, k: (0, chunk(t, k))),
            pl.BlockSpec((cols, D), lambda t, k: (chunk(t, k), 0)),
            _resident((1, D)),
            pl.BlockSpec((1, rows, D), lambda t, k: (0, 0, 0)),
        ],
        out_specs=pl.BlockSpec((1, rows, D), row_block),
        out_shape=jax.ShapeDtypeStruct((B, S, D), F32),
        scratch_shapes=[pltpu.VMEM((rows, D), F32)],
        compiler_params=pltpu.CompilerParams(
            dimension_semantics=("arbitrary", "arbitrary"),
            vmem_limit_bytes=V7X_VMEM_LIMIT_BYTES),
        name="ffn",
    )(f, h1, w1b, w2b, g, first)


def kernel(x, g_mix_pre, w_in, b_glu, w_dw, b_dw, ln_conv_g, ln_conv_b, w_out, g_mix_post,
           g_ffn_pre, w_ff1, w_ff2, g_ffn_post):
    B, S, D = x.shape
    depth = w_in.shape[0]
    assert S % IN_PROJ_ROWS == 0 and S % MIX_ROWS == 0 and S % FFN_ROWS == 0
    assert IN_PROJ_ROWS % MOBA_BLOCK == 0 and MIX_ROWS % CONV_HALO == 0
    assert MIX_ROWS % MIX_MATMUL_ROWS == 0 and MIX_MATMUL_ROWS % CONV_ROW_CHUNK == 0
    assert CONV_HALO >= CONV_WIDTH - 1 and CONV_HALO % SUBLANES == 0
    row = lambda v: v.reshape(1, -1)

    h = x
    for l in range(depth):
        w_q = w_in[l, :, :ATTN_WIDTH]
        w_k = w_in[l, :, ATTN_WIDTH:2 * ATTN_WIDTH].astype(BF16)
        w_v = w_in[l, :, 2 * ATTN_WIDTH:3 * ATTN_WIDTH]
        w_qvt = jnp.concatenate([w_q, w_v], axis=1).T.astype(BF16)
        w_glu = w_in[l, :, 3 * ATTN_WIDTH:].astype(BF16)

        k, qvt, kmean, hglu = _in_proj(h, row(g_mix_pre[l]), w_k, w_qvt, w_glu, row(b_glu[l]))
        kmean = kmean.reshape(B, S // MOBA_BLOCK, ATTN_WIDTH)
        attn = _moba(k, qvt, kmean)
        h1, f = _mix(h, attn, hglu, w_dw[l], row(b_dw[l]), row(ln_conv_g[l]), row(ln_conv_b[l]),
                     w_out[l].astype(BF16), row(g_mix_post[l]), row(g_ffn_pre[l]))
        h = _ffn(f, h1, w_ff1[l], w_ff2[l], row(g_ffn_post[l]))
    return h
```

```python
import functools

import numpy as np
import jax
import jax.numpy as jnp
from jax import lax
from jax.experimental import pallas as pl
from jax.experimental.pallas import tpu as pltpu

F32 = jnp.float32
BF16 = jnp.bfloat16

HEAD_DIM = 128
N_HEADS = 8
ATTN_WIDTH = N_HEADS * HEAD_DIM
MOBA_BLOCK = 256
MOBA_TOPK = 3
CONV_WIDTH = 31
RMS_EPS = 1e-6
LN_EPS = 1e-5

V7X_VMEM_LIMIT_BYTES = 60000 * 1024
SUBLANES = 8

IN_PROJ_ROWS = 512
IN_PROJ_COLS = 512
MIX_ROWS = 512
MIX_MATMUL_ROWS = 256
MIX_COLS = 512
CONV_HALO = 32
CONV_ROW_CHUNK = 128
CONV_LANE_CHUNK = 128
LN_ROW_CHUNK = 64
ATTN_UNROLL = 4
ATTN_BODY_STEPS = 4
FFN_ROWS = 512
FFN_COLS = 1024

LOG2E = float(np.log2(np.e))
Q_SCALE = HEAD_DIM ** -0.5 * LOG2E
ALIBI_PARTS = 3
V_PAD_ROWS = 16
MASKED = -1e30
NT_DIMS = (((1,), (1,)), ((), ()))


def _resident(shape):
    return pl.BlockSpec(shape, lambda *_: (0,) * len(shape), pipeline_mode=pl.Buffered(1))


def _alibi_slopes(n_heads):
    return (2.0 ** (-8.0 * np.arange(1, n_heads + 1) / n_heads)).astype(np.float32)


def _in_proj_kernel(x_ref, g_ref, w_ref, bglu_ref, k_ref, qvt_ref, kmean_ref, h_ref, wqvt_ref, *, scale):
    rows = x_ref.shape[1]
    conv_ch = h_ref.shape[2]
    k_col, v_col, glu_col = ATTN_WIDTH, 2 * ATTN_WIDTH, 3 * ATTN_WIDTH

    @pl.when(jnp.logical_and(pl.program_id(0) == 0, pl.program_id(1) == 0))
    def _():
        for rc in range(0, 2 * ATTN_WIDTH, MOBA_BLOCK):
            col = rc if rc < ATTN_WIDTH else v_col + rc - ATTN_WIDTH
            wqvt_ref[rc:rc + MOBA_BLOCK, :] = w_ref[:, col:col + MOBA_BLOCK].astype(F32).T.astype(BF16)

    x = x_ref[0]
    ms = jnp.mean(x * x, axis=-1, keepdims=True)
    a = (x * lax.rsqrt(ms + RMS_EPS) * g_ref[...]).astype(BF16)

    for c in range(0, ATTN_WIDTH, IN_PROJ_COLS):
        r = jnp.dot(a, w_ref[:, k_col + c:k_col + c + IN_PROJ_COLS], preferred_element_type=F32)
        blocks = r.reshape(rows // MOBA_BLOCK, MOBA_BLOCK, IN_PROJ_COLS)
        kmean_ref[0, 0, :, c:c + IN_PROJ_COLS] = jnp.mean(blocks, axis=1)
        k_ref[0, :, c:c + IN_PROJ_COLS] = r.astype(BF16)

    for rc in range(0, 2 * ATTN_WIDTH, MOBA_BLOCK):
        r = lax.dot_general(wqvt_ref[rc:rc + MOBA_BLOCK, :], a, NT_DIMS,
                            preferred_element_type=F32)
        if rc < ATTN_WIDTH:
            r = r * scale
        r = r.astype(BF16)
        for blk in range(rows // MOBA_BLOCK):
            qvt_ref[0, blk, rc:rc + MOBA_BLOCK, :] = r[:, blk * MOBA_BLOCK:(blk + 1) * MOBA_BLOCK]

    for c in range(0, conv_ch, IN_PROJ_COLS):
        val = jnp.dot(a, w_ref[:, glu_col + c:glu_col + c + IN_PROJ_COLS], preferred_element_type=F32)
        val = val + bglu_ref[:, c:c + IN_PROJ_COLS]
        gate_col = glu_col + conv_ch + c
        gt = jnp.dot(a, w_ref[:, gate_col:gate_col + IN_PROJ_COLS], preferred_element_type=F32)
        gt = gt + bglu_ref[:, conv_ch + c:conv_ch + c + IN_PROJ_COLS]
        h_ref[0, :, c:c + IN_PROJ_COLS] = val * (1.0 / (1.0 + jnp.exp(-gt)))


def _in_proj(x, g, w, b_glu):
    B, S, D = x.shape
    conv_ch = b_glu.shape[1] // 2
    assert w.shape == (D, 3 * ATTN_WIDTH + 2 * conv_ch)
    rows = IN_PROJ_ROWS
    n_blk = rows // MOBA_BLOCK
    grid = (B, S // rows)
    out_shape = (
        jax.ShapeDtypeStruct((B, S, ATTN_WIDTH), BF16),
        jax.ShapeDtypeStruct((B, S // MOBA_BLOCK, 2 * ATTN_WIDTH, MOBA_BLOCK), BF16),
        jax.ShapeDtypeStruct((B, S // rows, n_blk, ATTN_WIDTH), F32),
        jax.ShapeDtypeStruct((B, S, conv_ch), F32),
    )
    return pl.pallas_call(
        functools.partial(_in_proj_kernel, scale=Q_SCALE),
        grid=grid,
        in_specs=[
            pl.BlockSpec((1, rows, D), lambda b, i: (b, i, 0)),
            _resident((1, D)),
            _resident(w.shape),
            _resident((1, 2 * conv_ch)),
        ],
        out_specs=(
            pl.BlockSpec((1, rows, ATTN_WIDTH), lambda b, i: (b, i, 0)),
            pl.BlockSpec((1, n_blk, 2 * ATTN_WIDTH, MOBA_BLOCK), lambda b, i: (b, i, 0, 0)),
            pl.BlockSpec((1, 1, n_blk, ATTN_WIDTH), lambda b, i: (b, i, 0, 0)),
            pl.BlockSpec((1, rows, conv_ch), lambda b, i: (b, i, 0)),
        ),
        out_shape=out_shape,
        scratch_shapes=[pltpu.VMEM((2 * ATTN_WIDTH, D), BF16)],
        compiler_params=pltpu.CompilerParams(
            dimension_semantics=("arbitrary", "arbitrary"),
            vmem_limit_bytes=V7X_VMEM_LIMIT_BYTES),
        name="in_proj",
    )(x, g, w, b_glu)


def _pair_schedule(n_blocks, unroll):
    todo = {qi: list(range(qi)) for qi in range(1, n_blocks)}
    key_blocks, query_tiles = [], []
    while any(todo.values()):
        tiles = sorted((qi for qi in todo if todo[qi]), key=lambda t: -len(todo[t]))[:unroll]
        for qi in tiles:
            key_blocks.append(todo[qi].pop(0))
            query_tiles.append(qi)
        for _ in range(unroll - len(tiles)):
            key_blocks.append(0)
            query_tiles.append(n_blocks)
    while (len(key_blocks) // unroll) % ATTN_BODY_STEPS:
        key_blocks += [0] * unroll
        query_tiles += [n_blocks] * unroll
    return np.asarray(key_blocks, np.int32), np.asarray(query_tiles, np.int32)


def _moba_kernel(slopes_ref, pair_k_ref, pair_q_ref, qext_ref, qt_ref, k_ref, vt_ref, kmean_ref, o_ref,
                 qaug_ref, kaug_ref, vaug_ref, pen_ref, m_ref, acc_ref,
                 s_even_ref, s_odd_ref, smax_even_ref, smax_odd_ref):
    L = MOBA_BLOCK
    U = ATTN_UNROLL
    Dh = HEAD_DIM
    n_blocks = vt_ref.shape[1]
    n_steps = pair_k_ref.shape[0] // U
    slope = slopes_ref[pl.program_id(1)]

    lane = lax.broadcasted_iota(jnp.int32, (L, Dh), 1)
    key_cols = jnp.where(lane < ALIBI_PARTS, lax.broadcasted_iota(jnp.int32, (L, Dh), 0), 0).astype(BF16)
    ones_row = (lax.broadcasted_iota(jnp.int32, (V_PAD_ROWS, L), 0) == 0).astype(BF16)
    for blk in range(n_blocks):
        kaug_ref[blk * L:(blk + 1) * L, 0:Dh] = k_ref[0, blk * L:(blk + 1) * L, :]
        kaug_ref[blk * L:(blk + 1) * L, Dh:2 * Dh] = key_cols
        qaug_ref[blk, 0:Dh, :] = qt_ref[0, blk]
        qaug_ref[blk, Dh:2 * Dh, :] = qext_ref[0]
        vaug_ref[blk, 0:Dh, :] = vt_ref[0, blk]
        vaug_ref[blk, Dh:Dh + V_PAD_ROWS, :] = ones_row

    key_local = lax.broadcasted_iota(jnp.int32, (L, L), 0)
    qry_local = lax.broadcasted_iota(jnp.int32, (L, L), 1)
    kmean = kmean_ref[0].astype(BF16)
    blk_id = lax.broadcasted_iota(jnp.int32, (n_blocks, L), 0).astype(F32)

    def init_tiles(g, carry):
        tiles = [g * U + u for u in range(U)]
        for u, qi in enumerate(tiles):
            k_own = kaug_ref[pl.ds(pl.multiple_of(qi * L, L), L), :]
            s_even_ref[u] = jnp.dot(k_own, qaug_ref[qi], preferred_element_type=F32)
        gates = [jnp.dot(kmean, qt_ref[0, qi], preferred_element_type=F32) for qi in tiles]
        for u, qi in enumerate(tiles):
            gate = jnp.where(blk_id < jnp.asarray(qi, F32), gates[u], -jnp.inf)
            sel = jnp.zeros((n_blocks, L), F32)
            for r in range(MOBA_TOPK):
                best = jnp.max(gate, axis=0, keepdims=True)
                first = jnp.min(jnp.where(gate == best, blk_id, float(n_blocks)), axis=0, keepdims=True)
                hit = blk_id == first
                sel = jnp.maximum(sel, jnp.where(hit, jnp.asarray(qi > r, F32), 0.0))
                gate = jnp.where(hit, -jnp.inf, gate)
            pen_ref[qi] = jnp.where(sel > 0.0, 0.0, MASKED)

            s = jnp.where(key_local <= qry_local, s_even_ref[u], -jnp.inf)
            m = jnp.max(s, axis=0, keepdims=True)
            p = jnp.exp2(s - m)
            m_ref[qi] = m
            acc_ref[qi] = jnp.dot(vaug_ref[qi], p.astype(BF16), preferred_element_type=F32)
        return carry

    lax.fori_loop(0, n_blocks // U, init_tiles, 0)
    pen_ref[n_blocks] = jnp.full((n_blocks, L), MASKED, F32)
    m_ref[n_blocks] = jnp.zeros((1, L), F32)
    acc_ref[n_blocks] = jnp.zeros(acc_ref.shape[1:], F32)

    def scores(step, s_ref, smax_ref):
        for u in range(U):
            j = pair_k_ref[step * U + u]
            qi = jnp.minimum(pair_q_ref[step * U + u], n_blocks - 1)
            k_blk = kaug_ref[pl.ds(pl.multiple_of(j * L, L), L), :]
            s = jnp.dot(k_blk, qaug_ref[qi], preferred_element_type=F32)
            s_ref[u] = s
            smax_ref[u] = jnp.max(s, axis=0, keepdims=True)

    def update(step, s_ref, smax_ref):
        loaded = []
        for u in range(U):
            j = pair_k_ref[step * U + u]
            slot = pair_q_ref[step * U + u]
            row = pen_ref[slot, pl.ds(j, 1), :] - slope * float(L) * jnp.asarray(slot - j, F32)
            loaded.append((u, j, slot, row, m_ref[slot]))
        updated = []
        for u, j, slot, row, m in loaded:
            m_new = jnp.maximum(m, smax_ref[u] + row)
            alpha = jnp.exp2(m - m_new)
            p = jnp.exp2(s_ref[u] + (row - m_new))
            pv = jnp.dot(vaug_ref[j], p.astype(BF16), preferred_element_type=F32)
            acc_ref[slot] = alpha * acc_ref[slot] + pv
            updated.append((slot, m_new))
        for slot, m in updated:
            m_ref[slot] = m

    def several_steps(t, carry):
        for i in range(ATTN_BODY_STEPS):
            step = ATTN_BODY_STEPS * t + i
            cur, nxt = (even, odd) if i % 2 == 0 else (odd, even)
            scores(jnp.minimum(step + 1, n_steps - 1), *nxt)
            update(step, *cur)
        return carry

    even, odd = (s_even_ref, smax_even_ref), (s_odd_ref, smax_odd_ref)
    scores(0, *even)
    lax.fori_loop(0, n_steps // ATTN_BODY_STEPS, several_steps, 0)

    def finish_tiles(g, carry):
        for u in range(U):
            qi = g * U + u
            q0 = pl.multiple_of(qi * L, L)
            acc = acc_ref[qi]
            o_ref[0, pl.ds(q0, L), :] = (acc[0:Dh] / acc[Dh:Dh + 1]).T.astype(o_ref.dtype)
        return carry

    lax.fori_loop(0, n_blocks // U, finish_tiles, 0)


def _bf16_parts(x, n):
    parts, rest = [], np.asarray(x, np.float32)
    for _ in range(n):
        part = rest.astype(BF16).astype(np.float32)
        parts.append(part)
        rest = (rest - part).astype(np.float32)
    return np.stack(parts, axis=-1)


def _moba(k, qvt, kmean):
    B, S, _ = k.shape
    n_blocks = S // MOBA_BLOCK
    assert n_blocks % ATTN_UNROLL == 0 and ATTN_BODY_STEPS % 2 == 0
    pair_k, pair_q = _pair_schedule(n_blocks, ATTN_UNROLL)
    slopes = (_alibi_slopes(N_HEADS) * LOG2E).astype(np.float32)
    qext = np.zeros((N_HEADS, HEAD_DIM, MOBA_BLOCK), np.float32)
    qext[:, :ALIBI_PARTS, :] = _bf16_parts(slopes, ALIBI_PARTS)[:, :, None]
    smem = pl.BlockSpec(memory_space=pltpu.SMEM)
    return pl.pallas_call(
        _moba_kernel,
        grid=(B, N_HEADS),
        in_specs=[
            smem, smem, smem,
            pl.BlockSpec((1, HEAD_DIM, MOBA_BLOCK), lambda b, h: (h, 0, 0)),
            pl.BlockSpec((1, n_blocks, HEAD_DIM, MOBA_BLOCK), lambda b, h: (b, 0, h, 0)),
            pl.BlockSpec((1, S, HEAD_DIM), lambda b, h: (b, 0, h)),
            pl.BlockSpec((1, n_blocks, HEAD_DIM, MOBA_BLOCK), lambda b, h: (b, 0, N_HEADS + h, 0)),
            pl.BlockSpec((1, n_blocks, HEAD_DIM), lambda b, h: (b, 0, h)),
        ],
        out_specs=pl.BlockSpec((1, S, HEAD_DIM), lambda b, h: (b, 0, h)),
        out_shape=jax.ShapeDtypeStruct((B, S, ATTN_WIDTH), BF16),
        scratch_shapes=[
            pltpu.VMEM((n_blocks, 2 * HEAD_DIM, MOBA_BLOCK), BF16),
            pltpu.VMEM((S, 2 * HEAD_DIM), BF16),
            pltpu.VMEM((n_blocks, HEAD_DIM + V_PAD_ROWS, MOBA_BLOCK), BF16),
            pltpu.VMEM((n_blocks + 1, n_blocks, MOBA_BLOCK), F32),
            pltpu.VMEM((n_blocks + 1, 1, MOBA_BLOCK), F32),
            pltpu.VMEM((n_blocks + 1, HEAD_DIM + V_PAD_ROWS, MOBA_BLOCK), F32),
            pltpu.VMEM((ATTN_UNROLL, MOBA_BLOCK, MOBA_BLOCK), F32),
            pltpu.VMEM((ATTN_UNROLL, MOBA_BLOCK, MOBA_BLOCK), F32),
            pltpu.VMEM((ATTN_UNROLL, 1, MOBA_BLOCK), F32),
            pltpu.VMEM((ATTN_UNROLL, 1, MOBA_BLOCK), F32),
        ],
        compiler_params=pltpu.CompilerParams(
            dimension_semantics=("arbitrary", "arbitrary"),
            vmem_limit_bytes=V7X_VMEM_LIMIT_BYTES),
        name="moba",
    )(jnp.asarray(slopes), jnp.asarray(pair_k), jnp.asarray(pair_q), jnp.asarray(qext, BF16),
      qvt, k, qvt, kmean)


def _mix_kernel(x_ref, attn_ref, hmain_ref, hhalo_ref, wdw_ref, bdw_ref, lng_ref, lnb_ref,
                wo_ref, gpost_ref, gpre_ref, h1_ref, f_ref, ext_ref, y_ref, conv_ref, mixed_ref):
    rows = x_ref.shape[1]
    conv_ch = hmain_ref.shape[2]
    first_tile = pl.program_id(1) == 0

    ext_ref[0:CONV_HALO, :] = jnp.where(first_tile, 0.0, hhalo_ref[0])
    ext_ref[CONV_HALO:CONV_HALO + rows, :] = hmain_ref[0]

    attn = attn_ref[0]
    attn_w = attn.shape[1]
    d_model = wo_ref.shape[1]
    for c in range(0, d_model, MIX_COLS):
        mixed_ref[:, c:c + MIX_COLS] = jnp.dot(attn, wo_ref[0:attn_w, c:c + MIX_COLS],
                                               preferred_element_type=F32)

    shift0 = CONV_HALO - (CONV_WIDTH - 1)
    R, CL = CONV_ROW_CHUNK, CONV_LANE_CHUNK
    for r0 in range(0, rows, MIX_MATMUL_ROWS):
        for rc in range(r0, r0 + MIX_MATMUL_ROWS, R):
            for c0 in range(0, conv_ch, CL):
                y = None
                for b in range(SUBLANES):
                    z_rows = R if b == 0 else R + SUBLANES
                    z = None
                    for a in range((shift0 + CONV_WIDTH - 1) // SUBLANES + 1):
                        w = SUBLANES * a + b - shift0
                        if 0 <= w < CONV_WIDTH:
                            lo = rc + SUBLANES * a
                            tap = ext_ref[lo:lo + z_rows, c0:c0 + CL] * wdw_ref[w:w + 1, c0:c0 + CL]
                            z = tap if z is None else z + tap
                    part = z[b:b + R]
                    y = part if y is None else y + part
                y_ref[rc:rc + R, c0:c0 + CL] = y + bdw_ref[:, c0:c0 + CL]

        for rl in range(r0, r0 + MIX_MATMUL_ROWS, LN_ROW_CHUNK):
            y = y_ref[rl:rl + LN_ROW_CHUNK, :]
            mu = jnp.mean(y, axis=-1, keepdims=True)
            var = jnp.mean(jnp.square(y - mu), axis=-1, keepdims=True)
            y = (y - mu) * lax.rsqrt(var + LN_EPS) * lng_ref[...] + lnb_ref[...]
            y = y * (1.0 / (1.0 + jnp.exp(-y)))
            conv_ref[rl:rl + LN_ROW_CHUNK, :] = y.astype(BF16)

        conv = conv_ref[r0:r0 + MIX_MATMUL_ROWS, :]
        cols = []
        for c in range(0, d_model, MIX_COLS):
            r = jnp.dot(conv, wo_ref[attn_w:attn_w + conv_ch, c:c + MIX_COLS], preferred_element_type=F32)
            cols.append(mixed_ref[r0:r0 + MIX_MATMUL_ROWS, c:c + MIX_COLS] + r)
        mixed = jnp.concatenate(cols, axis=-1)
        ms = jnp.mean(mixed * mixed, axis=-1, keepdims=True)
        h1 = x_ref[0, r0:r0 + MIX_MATMUL_ROWS, :] + mixed * lax.rsqrt(ms + RMS_EPS) * gpost_ref[...]
        h1_ref[0, r0:r0 + MIX_MATMUL_ROWS, :] = h1
        ms1 = jnp.mean(h1 * h1, axis=-1, keepdims=True)
        f_ref[0, r0:r0 + MIX_MATMUL_ROWS, :] = (h1 * lax.rsqrt(ms1 + RMS_EPS) * gpre_ref[...]).astype(BF16)


def _mix(x, attn, hglu, w_dw, b_dw, ln_g, ln_b, w_out, g_post, g_pre):
    B, S, D = x.shape
    conv_ch = hglu.shape[2]
    rows = MIX_ROWS
    halo_per_tile = rows // CONV_HALO
    return pl.pallas_call(
        _mix_kernel,
        grid=(B, S // rows),
        in_specs=[
            pl.BlockSpec((1, rows, D), lambda b, i: (b, i, 0)),
            pl.BlockSpec((1, rows, attn.shape[2]), lambda b, i: (b, i, 0)),
            pl.BlockSpec((1, rows, conv_ch), lambda b, i: (b, i, 0)),
            pl.BlockSpec((1, CONV_HALO, conv_ch),
                         lambda b, i: (b, jnp.maximum(i * halo_per_tile - 1, 0), 0)),
            _resident(w_dw.shape),
            _resident((1, conv_ch)),
            _resident((1, conv_ch)),
            _resident((1, conv_ch)),
            _resident(w_out.shape),
            _resident((1, D)),
            _resident((1, D)),
        ],
        out_specs=(
            pl.BlockSpec((1, rows, D), lambda b, i: (b, i, 0)),
            pl.BlockSpec((1, rows, D), lambda b, i: (b, i, 0)),
        ),
        out_shape=(jax.ShapeDtypeStruct((B, S, D), F32), jax.ShapeDtypeStruct((B, S, D), BF16)),
        scratch_shapes=[
            pltpu.VMEM((CONV_HALO + rows, conv_ch), F32),
            pltpu.VMEM((rows, conv_ch), F32),
            pltpu.VMEM((rows, conv_ch), BF16),
            pltpu.VMEM((rows, D), F32),
        ],
        compiler_params=pltpu.CompilerParams(
            dimension_semantics=("arbitrary", "arbitrary"),
            vmem_limit_bytes=V7X_VMEM_LIMIT_BYTES),
        name="mix",
    )(x, attn, hglu, hglu, w_dw, b_dw, ln_g, ln_b, w_out, g_post, g_pre)


def _ffn_kernel(f_ref, h1_ref, w1_ref, w2_ref, g_ref, o_ref, acc_ref):
    kf = pl.program_id(2)

    @pl.when(kf == 0)
    def _():
        acc_ref[...] = jnp.zeros_like(acc_ref)

    u = jnp.dot(f_ref[0], w1_ref[...], preferred_element_type=F32)
    u = jnp.square(jnp.maximum(u, 0.0)).astype(BF16)
    acc_ref[...] += jnp.dot(u, w2_ref[...], preferred_element_type=F32)

    @pl.when(kf == pl.num_programs(2) - 1)
    def _():
        a = acc_ref[...]
        ms = jnp.mean(a * a, axis=-1, keepdims=True)
        o_ref[0] = h1_ref[0] + a * lax.rsqrt(ms + RMS_EPS) * g_ref[...]


def _ffn(f, h1, w1, w2, g):
    B, S, D = h1.shape
    d_ff = w1.shape[1]
    rows, cols = FFN_ROWS, FFN_COLS
    return pl.pallas_call(
        _ffn_kernel,
        grid=(B, S // rows, d_ff // cols),
        in_specs=[
            pl.BlockSpec((1, rows, D), lambda b, i, k: (b, i, 0)),
            pl.BlockSpec((1, rows, D), lambda b, i, k: (b, i, 0)),
            pl.BlockSpec((D, cols), lambda b, i, k: (0, k)),
            pl.BlockSpec((cols, D), lambda b, i, k: (k, 0)),
            _resident((1, D)),
        ],
        out_specs=pl.BlockSpec((1, rows, D), lambda b, i, k: (b, i, 0)),
        out_shape=jax.ShapeDtypeStruct((B, S, D), F32),
        scratch_shapes=[pltpu.VMEM((rows, D), F32)],
        compiler_params=pltpu.CompilerParams(
            dimension_semantics=("arbitrary", "arbitrary", "arbitrary"),
            vmem_limit_bytes=V7X_VMEM_LIMIT_BYTES),
        name="ffn",
    )(f, h1, w1, w2, g)


def kernel(x, g_mix_pre, w_in, b_glu, w_dw, b_dw, ln_conv_g, ln_conv_b, w_out, g_mix_post,
           g_ffn_pre, w_ff1, w_ff2, g_ffn_post):
    B, S, D = x.shape
    depth = w_in.shape[0]
    assert S % IN_PROJ_ROWS == 0 and S % MIX_ROWS == 0 and S % FFN_ROWS == 0
    assert IN_PROJ_ROWS % MOBA_BLOCK == 0 and MIX_ROWS % CONV_HALO == 0
    assert MIX_ROWS % MIX_MATMUL_ROWS == 0 and MIX_MATMUL_ROWS % CONV_ROW_CHUNK == 0
    assert CONV_HALO >= CONV_WIDTH - 1 and CONV_HALO % SUBLANES == 0
    row = lambda v: v.reshape(1, -1)

    h = x
    for l in range(depth):
        k, qvt, kmean, hglu = _in_proj(h, row(g_mix_pre[l]), w_in[l].astype(BF16), row(b_glu[l]))
        kmean = kmean.reshape(B, S // MOBA_BLOCK, ATTN_WIDTH)
        attn = _moba(k, qvt, kmean)
        h1, f = _mix(h, attn, hglu, w_dw[l], row(b_dw[l]), row(ln_conv_g[l]), row(ln_conv_b[l]),
                     w_out[l].astype(BF16), row(g_mix_post[l]), row(g_ffn_pre[l]))
        h = _ffn(f, h1, w_ff1[l].astype(BF16), w_ff2[l].astype(BF16), row(g_ffn_post[l]))
    return h
```

```python
import functools

import numpy as np
import jax
import jax.numpy as jnp
from jax import lax
from jax.experimental import pallas as pl
from jax.experimental.pallas import tpu as pltpu

F32 = jnp.float32
BF16 = jnp.bfloat16

HEAD_DIM = 128
N_HEADS = 8
ATTN_WIDTH = N_HEADS * HEAD_DIM
MOBA_BLOCK = 256
MOBA_TOPK = 3
CONV_WIDTH = 31
RMS_EPS = 1e-6
LN_EPS = 1e-5

V7X_VMEM_LIMIT_BYTES = 60000 * 1024
SUBLANES = 8
LANES = 128

IN_PROJ_ROWS = 512
IN_PROJ_COLS = 512
MIX_ROWS = 512
MIX_MATMUL_ROWS = 256
MIX_COLS = 512
CONV_HALO = 32
CONV_ROW_CHUNK = 128
CONV_LANE_CHUNK = 128
LN_ROW_CHUNK = 64
ATTN_UNROLL = 4
ATTN_BODY_STEPS = 4
FFN_ROWS = 1024
FFN_COLS = 512
FFN_NORM_ROWS = 256

LOG2E = float(np.log2(np.e))
Q_SCALE = HEAD_DIM ** -0.5 * LOG2E
ALIBI_PARTS = 3
V_PAD_ROWS = 16
MASKED = -1e30
NT_DIMS = (((1,), (1,)), ((), ()))


def _resident(shape):
    return pl.BlockSpec(shape, lambda *_: (0,) * len(shape), pipeline_mode=pl.Buffered(1))


def _alibi_slopes(n_heads):
    return (2.0 ** (-8.0 * np.arange(1, n_heads + 1) / n_heads)).astype(np.float32)


def _in_proj_kernel(x_ref, g_ref, w_ref, bglu_ref, k_ref, qvt_ref, kmean_ref, h_ref, wqvt_ref, *, scale):
    rows = x_ref.shape[1]
    conv_ch = h_ref.shape[2]
    k_col, v_col, glu_col = ATTN_WIDTH, 2 * ATTN_WIDTH, 3 * ATTN_WIDTH

    @pl.when(jnp.logical_and(pl.program_id(0) == 0, pl.program_id(1) == 0))
    def _():
        for rc in range(0, 2 * ATTN_WIDTH, MOBA_BLOCK):
            col = rc if rc < ATTN_WIDTH else v_col + rc - ATTN_WIDTH
            wqvt_ref[rc:rc + MOBA_BLOCK, :] = w_ref[:, col:col + MOBA_BLOCK].astype(F32).T.astype(BF16)

    x = x_ref[0]
    ms = jnp.mean(x * x, axis=-1, keepdims=True)
    a = (x * lax.rsqrt(ms + RMS_EPS) * g_ref[...]).astype(BF16)

    for c in range(0, ATTN_WIDTH, IN_PROJ_COLS):
        r = jnp.dot(a, w_ref[:, k_col + c:k_col + c + IN_PROJ_COLS], preferred_element_type=F32)
        blocks = r.reshape(rows // MOBA_BLOCK, MOBA_BLOCK, IN_PROJ_COLS)
        kmean_ref[0, 0, :, c:c + IN_PROJ_COLS] = jnp.mean(blocks, axis=1)
        k_ref[0, :, c:c + IN_PROJ_COLS] = r.astype(BF16)

    for rc in range(0, 2 * ATTN_WIDTH, MOBA_BLOCK):
        r = lax.dot_general(wqvt_ref[rc:rc + MOBA_BLOCK, :], a, NT_DIMS,
                            preferred_element_type=F32)
        if rc < ATTN_WIDTH:
            r = r * scale
        r = r.astype(BF16)
        for blk in range(rows // MOBA_BLOCK):
            qvt_ref[0, blk, rc:rc + MOBA_BLOCK, :] = r[:, blk * MOBA_BLOCK:(blk + 1) * MOBA_BLOCK]

    for c in range(0, conv_ch, IN_PROJ_COLS):
        val = jnp.dot(a, w_ref[:, glu_col + c:glu_col + c + IN_PROJ_COLS], preferred_element_type=F32)
        val = val + bglu_ref[:, c:c + IN_PROJ_COLS]
        gate_col = glu_col + conv_ch + c
        gt = jnp.dot(a, w_ref[:, gate_col:gate_col + IN_PROJ_COLS], preferred_element_type=F32)
        gt = gt + bglu_ref[:, conv_ch + c:conv_ch + c + IN_PROJ_COLS]
        h_ref[0, :, c:c + IN_PROJ_COLS] = val * (1.0 / (1.0 + jnp.exp(-gt)))


def _in_proj(x, g, w, b_glu):
    B, S, D = x.shape
    conv_ch = b_glu.shape[1] // 2
    assert w.shape == (D, 3 * ATTN_WIDTH + 2 * conv_ch)
    rows = IN_PROJ_ROWS
    n_blk = rows // MOBA_BLOCK
    grid = (B, S // rows)
    out_shape = (
        jax.ShapeDtypeStruct((B, S, ATTN_WIDTH), BF16),
        jax.ShapeDtypeStruct((B, S // MOBA_BLOCK, 2 * ATTN_WIDTH, MOBA_BLOCK), BF16),
        jax.ShapeDtypeStruct((B, S // rows, n_blk, ATTN_WIDTH), F32),
        jax.ShapeDtypeStruct((B, S, conv_ch), F32),
    )
    return pl.pallas_call(
        functools.partial(_in_proj_kernel, scale=Q_SCALE),
        grid=grid,
        in_specs=[
            pl.BlockSpec((1, rows, D), lambda b, i: (b, i, 0)),
            _resident((1, D)),
            _resident(w.shape),
            _resident((1, 2 * conv_ch)),
        ],
        out_specs=(
            pl.BlockSpec((1, rows, ATTN_WIDTH), lambda b, i: (b, i, 0)),
            pl.BlockSpec((1, n_blk, 2 * ATTN_WIDTH, MOBA_BLOCK), lambda b, i: (b, i, 0, 0)),
            pl.BlockSpec((1, 1, n_blk, ATTN_WIDTH), lambda b, i: (b, i, 0, 0)),
            pl.BlockSpec((1, rows, conv_ch), lambda b, i: (b, i, 0)),
        ),
        out_shape=out_shape,
        scratch_shapes=[pltpu.VMEM((2 * ATTN_WIDTH, D), BF16)],
        compiler_params=pltpu.CompilerParams(
            dimension_semantics=("arbitrary", "arbitrary"),
            vmem_limit_bytes=V7X_VMEM_LIMIT_BYTES),
        name="in_proj",
    )(x, g, w, b_glu)


def _pair_schedule(n_blocks, unroll):
    todo = {qi: list(range(qi)) for qi in range(1, n_blocks)}
    key_blocks, query_tiles = [], []
    while any(todo.values()):
        tiles = sorted((qi for qi in todo if todo[qi]), key=lambda t: -len(todo[t]))[:unroll]
        for qi in tiles:
            key_blocks.append(todo[qi].pop(0))
            query_tiles.append(qi)
        for _ in range(unroll - len(tiles)):
            key_blocks.append(0)
            query_tiles.append(n_blocks)
    while (len(key_blocks) // unroll) % ATTN_BODY_STEPS:
        key_blocks += [0] * unroll
        query_tiles += [n_blocks] * unroll
    return np.asarray(key_blocks, np.int32), np.asarray(query_tiles, np.int32)


def _moba_kernel(slopes_ref, pair_k_ref, pair_q_ref, qext_ref, qt_ref, k_ref, vt_ref, kmean_ref, o_ref,
                 qaug_ref, kaug_ref, vaug_ref, pen_ref, m_ref, acc_ref,
                 s_even_ref, s_odd_ref, smax_even_ref, smax_odd_ref):
    L = MOBA_BLOCK
    U = ATTN_UNROLL
    Dh = HEAD_DIM
    n_blocks = vt_ref.shape[1]
    n_steps = pair_k_ref.shape[0] // U
    slope = slopes_ref[pl.program_id(1)]

    lane = lax.broadcasted_iota(jnp.int32, (L, Dh), 1)
    key_cols = jnp.where(lane < ALIBI_PARTS, lax.broadcasted_iota(jnp.int32, (L, Dh), 0), 0).astype(BF16)
    ones_row = (lax.broadcasted_iota(jnp.int32, (V_PAD_ROWS, L), 0) == 0).astype(BF16)
    for blk in range(n_blocks):
        kaug_ref[blk * L:(blk + 1) * L, 0:Dh] = k_ref[0, blk * L:(blk + 1) * L, :]
        kaug_ref[blk * L:(blk + 1) * L, Dh:2 * Dh] = key_cols
        qaug_ref[blk, 0:Dh, :] = qt_ref[0, blk]
        qaug_ref[blk, Dh:2 * Dh, :] = qext_ref[0]
        vaug_ref[blk, 0:Dh, :] = vt_ref[0, blk]
        vaug_ref[blk, Dh:Dh + V_PAD_ROWS, :] = ones_row

    key_local = lax.broadcasted_iota(jnp.int32, (L, L), 0)
    qry_local = lax.broadcasted_iota(jnp.int32, (L, L), 1)
    kmean = kmean_ref[0].astype(BF16)
    blk_id = lax.broadcasted_iota(jnp.int32, (n_blocks, L), 0).astype(F32)

    def init_tiles(g, carry):
        tiles = [g * U + u for u in range(U)]
        for u, qi in enumerate(tiles):
            k_own = kaug_ref[pl.ds(pl.multiple_of(qi * L, L), L), :]
            s_even_ref[u] = jnp.dot(k_own, qaug_ref[qi], preferred_element_type=F32)
        gates = [jnp.dot(kmean, qt_ref[0, qi], preferred_element_type=F32) for qi in tiles]
        for u, qi in enumerate(tiles):
            gate = jnp.where(blk_id < jnp.asarray(qi, F32), gates[u], -jnp.inf)
            sel = jnp.zeros((n_blocks, L), F32)
            for r in range(MOBA_TOPK):
                best = jnp.max(gate, axis=0, keepdims=True)
                first = jnp.min(jnp.where(gate == best, blk_id, float(n_blocks)), axis=0, keepdims=True)
                hit = blk_id == first
                sel = jnp.maximum(sel, jnp.where(hit, jnp.asarray(qi > r, F32), 0.0))
                gate = jnp.where(hit, -jnp.inf, gate)
            pen_ref[qi] = jnp.where(sel > 0.0, 0.0, MASKED)

            s = jnp.where(key_local <= qry_local, s_even_ref[u], -jnp.inf)
            m = jnp.max(s, axis=0, keepdims=True)
            p = jnp.exp2(s - m)
            m_ref[qi] = m
            acc_ref[qi] = jnp.dot(vaug_ref[qi], p.astype(BF16), preferred_element_type=F32)
        return carry

    lax.fori_loop(0, n_blocks // U, init_tiles, 0)
    pen_ref[n_blocks] = jnp.full((n_blocks, L), MASKED, F32)
    m_ref[n_blocks] = jnp.zeros((1, L), F32)
    acc_ref[n_blocks] = jnp.zeros(acc_ref.shape[1:], F32)

    def scores(step, s_ref, smax_ref):
        for u in range(U):
            j = pair_k_ref[step * U + u]
            qi = jnp.minimum(pair_q_ref[step * U + u], n_blocks - 1)
            k_blk = kaug_ref[pl.ds(pl.multiple_of(j * L, L), L), :]
            s = jnp.dot(k_blk, qaug_ref[qi], preferred_element_type=F32)
            s_ref[u] = s
            smax_ref[u] = jnp.max(s, axis=0, keepdims=True)

    def update(step, s_ref, smax_ref):
        loaded = []
        for u in range(U):
            j = pair_k_ref[step * U + u]
            slot = pair_q_ref[step * U + u]
            row = pen_ref[slot, pl.ds(j, 1), :] - slope * float(L) * jnp.asarray(slot - j, F32)
            loaded.append((u, j, slot, row, m_ref[slot]))
        updated = []
        for u, j, slot, row, m in loaded:
            m_new = jnp.maximum(m, smax_ref[u] + row)
            alpha = jnp.exp2(m - m_new)
            p = jnp.exp2(s_ref[u] + (row - m_new))
            pv = jnp.dot(vaug_ref[j], p.astype(BF16), preferred_element_type=F32)
            acc_ref[slot] = alpha * acc_ref[slot] + pv
            updated.append((slot, m_new))
        for slot, m in updated:
            m_ref[slot] = m

    def several_steps(t, carry):
        for i in range(ATTN_BODY_STEPS):
            step = ATTN_BODY_STEPS * t + i
            cur, nxt = (even, odd) if i % 2 == 0 else (odd, even)
            scores(jnp.minimum(step + 1, n_steps - 1), *nxt)
            update(step, *cur)
        return carry

    even, odd = (s_even_ref, smax_even_ref), (s_odd_ref, smax_odd_ref)
    scores(0, *even)
    lax.fori_loop(0, n_steps // ATTN_BODY_STEPS, several_steps, 0)

    def finish_tiles(g, carry):
        for u in range(U):
            qi = g * U + u
            q0 = pl.multiple_of(qi * L, L)
            acc = acc_ref[qi]
            o_ref[0, pl.ds(q0, L), :] = (acc[0:Dh] / acc[Dh:Dh + 1]).T.astype(o_ref.dtype)
        return carry

    lax.fori_loop(0, n_blocks // U, finish_tiles, 0)


def _bf16_parts(x, n):
    parts, rest = [], np.asarray(x, np.float32)
    for _ in range(n):
        part = rest.astype(BF16).astype(np.float32)
        parts.append(part)
        rest = (rest - part).astype(np.float32)
    return np.stack(parts, axis=-1)


def _moba(k, qvt, kmean):
    B, S, _ = k.shape
    n_blocks = S // MOBA_BLOCK
    assert n_blocks % ATTN_UNROLL == 0 and ATTN_BODY_STEPS % 2 == 0
    pair_k, pair_q = _pair_schedule(n_blocks, ATTN_UNROLL)
    slopes = (_alibi_slopes(N_HEADS) * LOG2E).astype(np.float32)
    qext = np.zeros((N_HEADS, HEAD_DIM, MOBA_BLOCK), np.float32)
    qext[:, :ALIBI_PARTS, :] = _bf16_parts(slopes, ALIBI_PARTS)[:, :, None]
    smem = pl.BlockSpec(memory_space=pltpu.SMEM)
    return pl.pallas_call(
        _moba_kernel,
        grid=(B, N_HEADS),
        in_specs=[
            smem, smem, smem,
            pl.BlockSpec((1, HEAD_DIM, MOBA_BLOCK), lambda b, h: (h, 0, 0)),
            pl.BlockSpec((1, n_blocks, HEAD_DIM, MOBA_BLOCK), lambda b, h: (b, 0, h, 0)),
            pl.BlockSpec((1, S, HEAD_DIM), lambda b, h: (b, 0, h)),
            pl.BlockSpec((1, n_blocks, HEAD_DIM, MOBA_BLOCK), lambda b, h: (b, 0, N_HEADS + h, 0)),
            pl.BlockSpec((1, n_blocks, HEAD_DIM), lambda b, h: (b, 0, h)),
        ],
        out_specs=pl.BlockSpec((1, S, HEAD_DIM), lambda b, h: (b, 0, h)),
        out_shape=jax.ShapeDtypeStruct((B, S, ATTN_WIDTH), BF16),
        scratch_shapes=[
            pltpu.VMEM((n_blocks, 2 * HEAD_DIM, MOBA_BLOCK), BF16),
            pltpu.VMEM((S, 2 * HEAD_DIM), BF16),
            pltpu.VMEM((n_blocks, HEAD_DIM + V_PAD_ROWS, MOBA_BLOCK), BF16),
            pltpu.VMEM((n_blocks + 1, n_blocks, MOBA_BLOCK), F32),
            pltpu.VMEM((n_blocks + 1, 1, MOBA_BLOCK), F32),
            pltpu.VMEM((n_blocks + 1, HEAD_DIM + V_PAD_ROWS, MOBA_BLOCK), F32),
            pltpu.VMEM((ATTN_UNROLL, MOBA_BLOCK, MOBA_BLOCK), F32),
            pltpu.VMEM((ATTN_UNROLL, MOBA_BLOCK, MOBA_BLOCK), F32),
            pltpu.VMEM((ATTN_UNROLL, 1, MOBA_BLOCK), F32),
            pltpu.VMEM((ATTN_UNROLL, 1, MOBA_BLOCK), F32),
        ],
        compiler_params=pltpu.CompilerParams(
            dimension_semantics=("arbitrary", "arbitrary"),
            vmem_limit_bytes=V7X_VMEM_LIMIT_BYTES),
        name="moba",
    )(jnp.asarray(slopes), jnp.asarray(pair_k), jnp.asarray(pair_q), jnp.asarray(qext, BF16),
      qvt, k, qvt, kmean)


def _mix_kernel(x_ref, attn_ref, hmain_ref, hhalo_ref, wdw_ref, bdw_ref, lng_ref, lnb_ref,
                wo_ref, gpost_ref, gpre_ref, w1_ref, w2_ref,
                h1_ref, f_ref, w1b_ref, w2b_ref, ext_ref, y_ref, conv_ref, mixed_ref):
    rows = x_ref.shape[1]
    conv_ch = hmain_ref.shape[2]
    first_tile = pl.program_id(1) == 0

    w1b_ref[...] = w1_ref[...].astype(BF16)
    w2b_ref[...] = w2_ref[...].astype(BF16)

    ext_ref[0:CONV_HALO, :] = jnp.where(first_tile, 0.0, hhalo_ref[0])
    ext_ref[CONV_HALO:CONV_HALO + rows, :] = hmain_ref[0]

    attn = attn_ref[0]
    attn_w = attn.shape[1]
    d_model = wo_ref.shape[1]
    for c in range(0, d_model, MIX_COLS):
        mixed_ref[:, c:c + MIX_COLS] = jnp.dot(attn, wo_ref[0:attn_w, c:c + MIX_COLS],
                                               preferred_element_type=F32)

    shift0 = CONV_HALO - (CONV_WIDTH - 1)
    R, CL = CONV_ROW_CHUNK, CONV_LANE_CHUNK
    for r0 in range(0, rows, MIX_MATMUL_ROWS):
        for rc in range(r0, r0 + MIX_MATMUL_ROWS, R):
            for c0 in range(0, conv_ch, CL):
                y = None
                for b in range(SUBLANES):
                    z_rows = R if b == 0 else R + SUBLANES
                    z = None
                    for a in range((shift0 + CONV_WIDTH - 1) // SUBLANES + 1):
                        w = SUBLANES * a + b - shift0
                        if 0 <= w < CONV_WIDTH:
                            lo = rc + SUBLANES * a
                            tap = ext_ref[lo:lo + z_rows, c0:c0 + CL] * wdw_ref[w:w + 1, c0:c0 + CL]
                            z = tap if z is None else z + tap
                    part = z[b:b + R]
                    y = part if y is None else y + part
                y_ref[rc:rc + R, c0:c0 + CL] = y + bdw_ref[:, c0:c0 + CL]

        for rl in range(r0, r0 + MIX_MATMUL_ROWS, LN_ROW_CHUNK):
            y = y_ref[rl:rl + LN_ROW_CHUNK, :]
            mu = jnp.mean(y, axis=-1, keepdims=True)
            var = jnp.mean(jnp.square(y - mu), axis=-1, keepdims=True)
            y = (y - mu) * lax.rsqrt(var + LN_EPS) * lng_ref[...] + lnb_ref[...]
            y = y * (1.0 / (1.0 + jnp.exp(-y)))
            conv_ref[rl:rl + LN_ROW_CHUNK, :] = y.astype(BF16)

        conv = conv_ref[r0:r0 + MIX_MATMUL_ROWS, :]
        cols = []
        for c in range(0, d_model, MIX_COLS):
            r = jnp.dot(conv, wo_ref[attn_w:attn_w + conv_ch, c:c + MIX_COLS], preferred_element_type=F32)
            cols.append(mixed_ref[r0:r0 + MIX_MATMUL_ROWS, c:c + MIX_COLS] + r)
        mixed = jnp.concatenate(cols, axis=-1)
        ms = jnp.mean(mixed * mixed, axis=-1, keepdims=True)
        h1 = x_ref[0, r0:r0 + MIX_MATMUL_ROWS, :] + mixed * lax.rsqrt(ms + RMS_EPS) * gpost_ref[...]
        h1_ref[0, r0:r0 + MIX_MATMUL_ROWS, :] = h1
        ms1 = jnp.mean(h1 * h1, axis=-1, keepdims=True)
        f_ref[0, r0:r0 + MIX_MATMUL_ROWS, :] = (h1 * lax.rsqrt(ms1 + RMS_EPS) * gpre_ref[...]).astype(BF16)


def _mix(x, attn, hglu, w_dw, b_dw, ln_g, ln_b, w_out, g_post, g_pre, w_ff1, w_ff2):
    B, S, D = x.shape
    conv_ch = hglu.shape[2]
    rows = MIX_ROWS
    halo_per_tile = rows // CONV_HALO
    tiles_per_seq = S // rows
    d_ff = w_ff1.shape[1]
    ff_slice = d_ff // (B * tiles_per_seq)
    assert ff_slice * B * tiles_per_seq == d_ff and ff_slice % LANES == 0
    return pl.pallas_call(
        _mix_kernel,
        grid=(B, tiles_per_seq),
        in_specs=[
            pl.BlockSpec((1, rows, D), lambda b, i: (b, i, 0)),
            pl.BlockSpec((1, rows, attn.shape[2]), lambda b, i: (b, i, 0)),
            pl.BlockSpec((1, rows, conv_ch), lambda b, i: (b, i, 0)),
            pl.BlockSpec((1, CONV_HALO, conv_ch),
                         lambda b, i: (b, jnp.maximum(i * halo_per_tile - 1, 0), 0)),
            _resident(w_dw.shape),
            _resident((1, conv_ch)),
            _resident((1, conv_ch)),
            _resident((1, conv_ch)),
            _resident(w_out.shape),
            _resident((1, D)),
            _resident((1, D)),
            pl.BlockSpec((D, ff_slice), lambda b, i: (0, b * tiles_per_seq + i)),
            pl.BlockSpec((ff_slice, D), lambda b, i: (b * tiles_per_seq + i, 0)),
        ],
        out_specs=(
            pl.BlockSpec((1, rows, D), lambda b, i: (b, i, 0)),
            pl.BlockSpec((1, rows, D), lambda b, i: (b, i, 0)),
            pl.BlockSpec((D, ff_slice), lambda b, i: (0, b * tiles_per_seq + i)),
            pl.BlockSpec((ff_slice, D), lambda b, i: (b * tiles_per_seq + i, 0)),
        ),
        out_shape=(jax.ShapeDtypeStruct((B, S, D), F32), jax.ShapeDtypeStruct((B, S, D), BF16),
                   jax.ShapeDtypeStruct(w_ff1.shape, BF16), jax.ShapeDtypeStruct(w_ff2.shape, BF16)),
        scratch_shapes=[
            pltpu.VMEM((CONV_HALO + rows, conv_ch), F32),
            pltpu.VMEM((rows, conv_ch), F32),
            pltpu.VMEM((rows, conv_ch), BF16),
            pltpu.VMEM((rows, D), F32),
        ],
        compiler_params=pltpu.CompilerParams(
            dimension_semantics=("arbitrary", "arbitrary"),
            vmem_limit_bytes=V7X_VMEM_LIMIT_BYTES),
        name="mix",
    )(x, attn, hglu, hglu, w_dw, b_dw, ln_g, ln_b, w_out, g_post, g_pre, w_ff1, w_ff2)


def _ffn_kernel(f_ref, h1_ref, w1_ref, w2_ref, g_ref, o_ref):
    kf = pl.program_id(2)

    @pl.when(kf == 0)
    def _():
        o_ref[...] = jnp.zeros_like(o_ref)

    u = jnp.dot(f_ref[0], w1_ref[...], preferred_element_type=F32)
    u = jnp.square(jnp.maximum(u, 0.0)).astype(BF16)
    o_ref[0] += jnp.dot(u, w2_ref[...], preferred_element_type=F32)

    @pl.when(kf == pl.num_programs(2) - 1)
    def _():
        for r0 in range(0, o_ref.shape[1], FFN_NORM_ROWS):
            a = o_ref[0, r0:r0 + FFN_NORM_ROWS, :]
            ms = jnp.mean(a * a, axis=-1, keepdims=True)
            o_ref[0, r0:r0 + FFN_NORM_ROWS, :] = (h1_ref[0, r0:r0 + FFN_NORM_ROWS, :]
                                                  + a * lax.rsqrt(ms + RMS_EPS) * g_ref[...])


def _ffn(f, h1, w1, w2, g):
    B, S, D = h1.shape
    d_ff = w1.shape[1]
    rows, cols = FFN_ROWS, FFN_COLS
    return pl.pallas_call(
        _ffn_kernel,
        grid=(B, S // rows, d_ff // cols),
        in_specs=[
            pl.BlockSpec((1, rows, D), lambda b, i, k: (b, i, 0)),
            pl.BlockSpec((1, rows, D), lambda b, i, k: (b, i, 0)),
            pl.BlockSpec((D, cols), lambda b, i, k: (0, k)),
            pl.BlockSpec((cols, D), lambda b, i, k: (k, 0)),
            _resident((1, D)),
        ],
        out_specs=pl.BlockSpec((1, rows, D), lambda b, i, k: (b, i, 0)),
        out_shape=jax.ShapeDtypeStruct((B, S, D), F32),
        compiler_params=pltpu.CompilerParams(
            dimension_semantics=("arbitrary", "arbitrary", "arbitrary"),
            vmem_limit_bytes=V7X_VMEM_LIMIT_BYTES),
        name="ffn",
    )(f, h1, w1, w2, g)


def kernel(x, g_mix_pre, w_in, b_glu, w_dw, b_dw, ln_conv_g, ln_conv_b, w_out, g_mix_post,
           g_ffn_pre, w_ff1, w_ff2, g_ffn_post):
    B, S, D = x.shape
    depth = w_in.shape[0]
    assert S % IN_PROJ_ROWS == 0 and S % MIX_ROWS == 0 and S % FFN_ROWS == 0
    assert IN_PROJ_ROWS % MOBA_BLOCK == 0 and MIX_ROWS % CONV_HALO == 0
    assert MIX_ROWS % MIX_MATMUL_ROWS == 0 and MIX_MATMUL_ROWS % CONV_ROW_CHUNK == 0
    assert CONV_HALO >= CONV_WIDTH - 1 and CONV_HALO % SUBLANES == 0
    row = lambda v: v.reshape(1, -1)

    h = x
    for l in range(depth):
        k, qvt, kmean, hglu = _in_proj(h, row(g_mix_pre[l]), w_in[l].astype(BF16), row(b_glu[l]))
        kmean = kmean.reshape(B, S // MOBA_BLOCK, ATTN_WIDTH)
        attn = _moba(k, qvt, kmean)
        h1, f, w1b, w2b = _mix(h, attn, hglu, w_dw[l], row(b_dw[l]), row(ln_conv_g[l]), row(ln_conv_b[l]),
                               w_out[l].astype(BF16), row(g_mix_post[l]), row(g_ffn_pre[l]),
                               w_ff1[l], w_ff2[l])
        h = _ffn(f, h1, w1b, w2b, row(g_ffn_post[l]))
    return h
```

```python
import functools

import numpy as np
import jax
import jax.numpy as jnp
from jax import lax
from jax.experimental import pallas as pl
from jax.experimental.pallas import tpu as pltpu

F32 = jnp.float32
BF16 = jnp.bfloat16

HEAD_DIM = 128
N_HEADS = 8
ATTN_WIDTH = N_HEADS * HEAD_DIM
MOBA_BLOCK = 256
MOBA_TOPK = 3
CONV_WIDTH = 31
RMS_EPS = 1e-6
LN_EPS = 1e-5

V7X_VMEM_LIMIT_BYTES = 60000 * 1024
SUBLANES = 8
LANES = 128

IN_PROJ_ROWS = 512
IN_PROJ_COLS = 512
MIX_ROWS = 512
MIX_MATMUL_ROWS = 256
MIX_COLS = 512
CONV_HALO = 32
CONV_ROW_CHUNK = 128
CONV_LANE_CHUNK = 128
LN_ROW_CHUNK = 64
ATTN_UNROLL = 4
ATTN_BODY_STEPS = 4
FFN_ROWS = 512
FFN_COLS = 1024

LOG2E = float(np.log2(np.e))
Q_SCALE = HEAD_DIM ** -0.5 * LOG2E
ALIBI_PARTS = 3
V_PAD_ROWS = 16
MASKED = -1e30
NT_DIMS = (((1,), (1,)), ((), ()))


def _resident(shape):
    return pl.BlockSpec(shape, lambda *_: (0,) * len(shape), pipeline_mode=pl.Buffered(1))


def _alibi_slopes(n_heads):
    return (2.0 ** (-8.0 * np.arange(1, n_heads + 1) / n_heads)).astype(np.float32)


def _in_proj_kernel(x_ref, g_ref, w_ref, bglu_ref, wo_ref,
                    k_ref, qvt_ref, kmean_ref, h_ref, wob_ref, wqvt_ref, *, scale):
    rows = x_ref.shape[1]
    conv_ch = h_ref.shape[2]
    k_col, v_col, glu_col = ATTN_WIDTH, 2 * ATTN_WIDTH, 3 * ATTN_WIDTH

    wob_ref[...] = wo_ref[...].astype(BF16)

    @pl.when(jnp.logical_and(pl.program_id(0) == 0, pl.program_id(1) == 0))
    def _():
        for rc in range(0, 2 * ATTN_WIDTH, MOBA_BLOCK):
            col = rc if rc < ATTN_WIDTH else v_col + rc - ATTN_WIDTH
            wqvt_ref[rc:rc + MOBA_BLOCK, :] = w_ref[:, col:col + MOBA_BLOCK].astype(F32).T.astype(BF16)

    x = x_ref[0]
    ms = jnp.mean(x * x, axis=-1, keepdims=True)
    a = (x * lax.rsqrt(ms + RMS_EPS) * g_ref[...]).astype(BF16)

    for c in range(0, ATTN_WIDTH, IN_PROJ_COLS):
        r = jnp.dot(a, w_ref[:, k_col + c:k_col + c + IN_PROJ_COLS], preferred_element_type=F32)
        blocks = r.reshape(rows // MOBA_BLOCK, MOBA_BLOCK, IN_PROJ_COLS)
        kmean_ref[0, 0, :, c:c + IN_PROJ_COLS] = jnp.mean(blocks, axis=1)
        k_ref[0, :, c:c + IN_PROJ_COLS] = r.astype(BF16)

    for rc in range(0, 2 * ATTN_WIDTH, MOBA_BLOCK):
        r = lax.dot_general(wqvt_ref[rc:rc + MOBA_BLOCK, :], a, NT_DIMS,
                            preferred_element_type=F32)
        if rc < ATTN_WIDTH:
            r = r * scale
        r = r.astype(BF16)
        for blk in range(rows // MOBA_BLOCK):
            qvt_ref[0, blk, rc:rc + MOBA_BLOCK, :] = r[:, blk * MOBA_BLOCK:(blk + 1) * MOBA_BLOCK]

    for c in range(0, conv_ch, IN_PROJ_COLS):
        val = jnp.dot(a, w_ref[:, glu_col + c:glu_col + c + IN_PROJ_COLS], preferred_element_type=F32)
        val = val + bglu_ref[:, c:c + IN_PROJ_COLS]
        gate_col = glu_col + conv_ch + c
        gt = jnp.dot(a, w_ref[:, gate_col:gate_col + IN_PROJ_COLS], preferred_element_type=F32)
        gt = gt + bglu_ref[:, conv_ch + c:conv_ch + c + IN_PROJ_COLS]
        h_ref[0, :, c:c + IN_PROJ_COLS] = val * (1.0 / (1.0 + jnp.exp(-gt)))


def _in_proj(x, g, w, b_glu, w_out):
    B, S, D = x.shape
    conv_ch = b_glu.shape[1] // 2
    assert w.shape == (D, 3 * ATTN_WIDTH + 2 * conv_ch)
    rows = IN_PROJ_ROWS
    n_blk = rows // MOBA_BLOCK
    tiles_per_seq = S // rows
    grid = (B, tiles_per_seq)
    wo_slice = w_out.shape[0] // (B * tiles_per_seq)
    assert wo_slice * B * tiles_per_seq == w_out.shape[0] and wo_slice % (2 * SUBLANES) == 0
    out_shape = (
        jax.ShapeDtypeStruct((B, S, ATTN_WIDTH), BF16),
        jax.ShapeDtypeStruct((B, S // MOBA_BLOCK, 2 * ATTN_WIDTH, MOBA_BLOCK), BF16),
        jax.ShapeDtypeStruct((B, S // rows, n_blk, ATTN_WIDTH), F32),
        jax.ShapeDtypeStruct((B, S, conv_ch), F32),
        jax.ShapeDtypeStruct(w_out.shape, BF16),
    )
    return pl.pallas_call(
        functools.partial(_in_proj_kernel, scale=Q_SCALE),
        grid=grid,
        in_specs=[
            pl.BlockSpec((1, rows, D), lambda b, i: (b, i, 0)),
            _resident((1, D)),
            _resident(w.shape),
            _resident((1, 2 * conv_ch)),
            pl.BlockSpec((wo_slice, w_out.shape[1]), lambda b, i: (b * tiles_per_seq + i, 0)),
        ],
        out_specs=(
            pl.BlockSpec((1, rows, ATTN_WIDTH), lambda b, i: (b, i, 0)),
            pl.BlockSpec((1, n_blk, 2 * ATTN_WIDTH, MOBA_BLOCK), lambda b, i: (b, i, 0, 0)),
            pl.BlockSpec((1, 1, n_blk, ATTN_WIDTH), lambda b, i: (b, i, 0, 0)),
            pl.BlockSpec((1, rows, conv_ch), lambda b, i: (b, i, 0)),
            pl.BlockSpec((wo_slice, w_out.shape[1]), lambda b, i: (b * tiles_per_seq + i, 0)),
        ),
        out_shape=out_shape,
        scratch_shapes=[pltpu.VMEM((2 * ATTN_WIDTH, D), BF16)],
        compiler_params=pltpu.CompilerParams(
            dimension_semantics=("arbitrary", "arbitrary"),
            vmem_limit_bytes=V7X_VMEM_LIMIT_BYTES),
        name="in_proj",
    )(x, g, w, b_glu, w_out)


def _pair_schedule(n_blocks, unroll):
    todo = {qi: list(range(qi)) for qi in range(1, n_blocks)}
    key_blocks, query_tiles = [], []
    while any(todo.values()):
        tiles = sorted((qi for qi in todo if todo[qi]), key=lambda t: -len(todo[t]))[:unroll]
        for qi in tiles:
            key_blocks.append(todo[qi].pop(0))
            query_tiles.append(qi)
        for _ in range(unroll - len(tiles)):
            key_blocks.append(0)
            query_tiles.append(n_blocks)
    while (len(key_blocks) // unroll) % ATTN_BODY_STEPS:
        key_blocks += [0] * unroll
        query_tiles += [n_blocks] * unroll
    return np.asarray(key_blocks, np.int32), np.asarray(query_tiles, np.int32)


def _moba_kernel(slopes_ref, pair_k_ref, pair_q_ref, qext_ref, qt_ref, k_ref, vt_ref, kmean_ref, o_ref,
                 qaug_ref, kaug_ref, vaug_ref, pen_ref, m_ref, acc_ref,
                 s_even_ref, s_odd_ref, smax_even_ref, smax_odd_ref):
    L = MOBA_BLOCK
    U = ATTN_UNROLL
    Dh = HEAD_DIM
    n_blocks = vt_ref.shape[1]
    n_steps = pair_k_ref.shape[0] // U
    slope = slopes_ref[pl.program_id(1)]

    lane = lax.broadcasted_iota(jnp.int32, (L, Dh), 1)
    key_cols = jnp.where(lane < ALIBI_PARTS, lax.broadcasted_iota(jnp.int32, (L, Dh), 0), 0).astype(BF16)
    ones_row = (lax.broadcasted_iota(jnp.int32, (V_PAD_ROWS, L), 0) == 0).astype(BF16)
    for blk in range(n_blocks):
        kaug_ref[blk * L:(blk + 1) * L, 0:Dh] = k_ref[0, blk * L:(blk + 1) * L, :]
        kaug_ref[blk * L:(blk + 1) * L, Dh:2 * Dh] = key_cols
        qaug_ref[blk, 0:Dh, :] = qt_ref[0, blk]
        qaug_ref[blk, Dh:2 * Dh, :] = qext_ref[0]
        vaug_ref[blk, 0:Dh, :] = vt_ref[0, blk]
        vaug_ref[blk, Dh:Dh + V_PAD_ROWS, :] = ones_row

    key_local = lax.broadcasted_iota(jnp.int32, (L, L), 0)
    qry_local = lax.broadcasted_iota(jnp.int32, (L, L), 1)
    kmean = kmean_ref[0].astype(BF16)
    blk_id = lax.broadcasted_iota(jnp.int32, (n_blocks, L), 0).astype(F32)

    def init_tiles(g, carry):
        tiles = [g * U + u for u in range(U)]
        for u, qi in enumerate(tiles):
            k_own = kaug_ref[pl.ds(pl.multiple_of(qi * L, L), L), :]
            s_even_ref[u] = jnp.dot(k_own, qaug_ref[qi], preferred_element_type=F32)
        gates = [jnp.dot(kmean, qt_ref[0, qi], preferred_element_type=F32) for qi in tiles]
        for u, qi in enumerate(tiles):
            gate = jnp.where(blk_id < jnp.asarray(qi, F32), gates[u], -jnp.inf)
            sel = jnp.zeros((n_blocks, L), F32)
            for r in range(MOBA_TOPK):
                best = jnp.max(gate, axis=0, keepdims=True)
                first = jnp.min(jnp.where(gate == best, blk_id, float(n_blocks)), axis=0, keepdims=True)
                hit = blk_id == first
                sel = jnp.maximum(sel, jnp.where(hit, jnp.asarray(qi > r, F32), 0.0))
                gate = jnp.where(hit, -jnp.inf, gate)
            pen_ref[qi] = jnp.where(sel > 0.0, 0.0, MASKED)

            s = jnp.where(key_local <= qry_local, s_even_ref[u], -jnp.inf)
            m = jnp.max(s, axis=0, keepdims=True)
            p = jnp.exp2(s - m)
            m_ref[qi] = m
            acc_ref[qi] = jnp.dot(vaug_ref[qi], p.astype(BF16), preferred_element_type=F32)
        return carry

    lax.fori_loop(0, n_blocks // U, init_tiles, 0)
    pen_ref[n_blocks] = jnp.full((n_blocks, L), MASKED, F32)
    m_ref[n_blocks] = jnp.zeros((1, L), F32)
    acc_ref[n_blocks] = jnp.zeros(acc_ref.shape[1:], F32)

    def scores(step, s_ref, smax_ref):
        for u in range(U):
            j = pair_k_ref[step * U + u]
            qi = jnp.minimum(pair_q_ref[step * U + u], n_blocks - 1)
            k_blk = kaug_ref[pl.ds(pl.multiple_of(j * L, L), L), :]
            s = jnp.dot(k_blk, qaug_ref[qi], preferred_element_type=F32)
            s_ref[u] = s
            smax_ref[u] = jnp.max(s, axis=0, keepdims=True)

    def update(step, s_ref, smax_ref):
        loaded = []
        for u in range(U):
            j = pair_k_ref[step * U + u]
            slot = pair_q_ref[step * U + u]
            row = pen_ref[slot, pl.ds(j, 1), :] - slope * float(L) * jnp.asarray(slot - j, F32)
            loaded.append((u, j, slot, row, m_ref[slot]))
        updated = []
        for u, j, slot, row, m in loaded:
            m_new = jnp.maximum(m, smax_ref[u] + row)
            alpha = jnp.exp2(m - m_new)
            p = jnp.exp2(s_ref[u] + (row - m_new))
            pv = jnp.dot(vaug_ref[j], p.astype(BF16), preferred_element_type=F32)
            acc_ref[slot] = alpha * acc_ref[slot] + pv
            updated.append((slot, m_new))
        for slot, m in updated:
            m_ref[slot] = m

    def several_steps(t, carry):
        for i in range(ATTN_BODY_STEPS):
            step = ATTN_BODY_STEPS * t + i
            cur, nxt = (even, odd) if i % 2 == 0 else (odd, even)
            scores(jnp.minimum(step + 1, n_steps - 1), *nxt)
            update(step, *cur)
        return carry

    even, odd = (s_even_ref, smax_even_ref), (s_odd_ref, smax_odd_ref)
    scores(0, *even)
    lax.fori_loop(0, n_steps // ATTN_BODY_STEPS, several_steps, 0)

    def finish_tiles(g, carry):
        for u in range(U):
            qi = g * U + u
            q0 = pl.multiple_of(qi * L, L)
            acc = acc_ref[qi]
            o_ref[0, pl.ds(q0, L), :] = (acc[0:Dh] / acc[Dh:Dh + 1]).T.astype(o_ref.dtype)
        return carry

    lax.fori_loop(0, n_blocks // U, finish_tiles, 0)


def _bf16_parts(x, n):
    parts, rest = [], np.asarray(x, np.float32)
    for _ in range(n):
        part = rest.astype(BF16).astype(np.float32)
        parts.append(part)
        rest = (rest - part).astype(np.float32)
    return np.stack(parts, axis=-1)


def _moba(k, qvt, kmean):
    B, S, _ = k.shape
    n_blocks = S // MOBA_BLOCK
    assert n_blocks % ATTN_UNROLL == 0 and ATTN_BODY_STEPS % 2 == 0
    pair_k, pair_q = _pair_schedule(n_blocks, ATTN_UNROLL)
    slopes = (_alibi_slopes(N_HEADS) * LOG2E).astype(np.float32)
    qext = np.zeros((N_HEADS, HEAD_DIM, MOBA_BLOCK), np.float32)
    qext[:, :ALIBI_PARTS, :] = _bf16_parts(slopes, ALIBI_PARTS)[:, :, None]
    smem = pl.BlockSpec(memory_space=pltpu.SMEM)
    return pl.pallas_call(
        _moba_kernel,
        grid=(B, N_HEADS),
        in_specs=[
            smem, smem, smem,
            pl.BlockSpec((1, HEAD_DIM, MOBA_BLOCK), lambda b, h: (h, 0, 0)),
            pl.BlockSpec((1, n_blocks, HEAD_DIM, MOBA_BLOCK), lambda b, h: (b, 0, h, 0)),
            pl.BlockSpec((1, S, HEAD_DIM), lambda b, h: (b, 0, h)),
            pl.BlockSpec((1, n_blocks, HEAD_DIM, MOBA_BLOCK), lambda b, h: (b, 0, N_HEADS + h, 0)),
            pl.BlockSpec((1, n_blocks, HEAD_DIM), lambda b, h: (b, 0, h)),
        ],
        out_specs=pl.BlockSpec((1, S, HEAD_DIM), lambda b, h: (b, 0, h)),
        out_shape=jax.ShapeDtypeStruct((B, S, ATTN_WIDTH), BF16),
        scratch_shapes=[
            pltpu.VMEM((n_blocks, 2 * HEAD_DIM, MOBA_BLOCK), BF16),
            pltpu.VMEM((S, 2 * HEAD_DIM), BF16),
            pltpu.VMEM((n_blocks, HEAD_DIM + V_PAD_ROWS, MOBA_BLOCK), BF16),
            pltpu.VMEM((n_blocks + 1, n_blocks, MOBA_BLOCK), F32),
            pltpu.VMEM((n_blocks + 1, 1, MOBA_BLOCK), F32),
            pltpu.VMEM((n_blocks + 1, HEAD_DIM + V_PAD_ROWS, MOBA_BLOCK), F32),
            pltpu.VMEM((ATTN_UNROLL, MOBA_BLOCK, MOBA_BLOCK), F32),
            pltpu.VMEM((ATTN_UNROLL, MOBA_BLOCK, MOBA_BLOCK), F32),
            pltpu.VMEM((ATTN_UNROLL, 1, MOBA_BLOCK), F32),
            pltpu.VMEM((ATTN_UNROLL, 1, MOBA_BLOCK), F32),
        ],
        compiler_params=pltpu.CompilerParams(
            dimension_semantics=("arbitrary", "arbitrary"),
            vmem_limit_bytes=V7X_VMEM_LIMIT_BYTES),
        name="moba",
    )(jnp.asarray(slopes), jnp.asarray(pair_k), jnp.asarray(pair_q), jnp.asarray(qext, BF16),
      qvt, k, qvt, kmean)


def _mix_kernel(x_ref, attn_ref, hmain_ref, hhalo_ref, wdw_ref, bdw_ref, lng_ref, lnb_ref,
                wo_ref, gpost_ref, gpre_ref, w1_ref, w2_ref,
                h1_ref, f_ref, w1b_ref, w2b_ref, ext_ref, y_ref, conv_ref, mixed_ref):
    rows = x_ref.shape[1]
    conv_ch = hmain_ref.shape[2]
    first_tile = pl.program_id(1) == 0

    w1b_ref[...] = w1_ref[...].astype(BF16)
    w2b_ref[...] = w2_ref[...].astype(BF16)

    ext_ref[0:CONV_HALO, :] = jnp.where(first_tile, 0.0, hhalo_ref[0])
    ext_ref[CONV_HALO:CONV_HALO + rows, :] = hmain_ref[0]

    attn = attn_ref[0]
    attn_w = attn.shape[1]
    d_model = wo_ref.shape[1]
    for c in range(0, d_model, MIX_COLS):
        mixed_ref[:, c:c + MIX_COLS] = jnp.dot(attn, wo_ref[0:attn_w, c:c + MIX_COLS],
                                               preferred_element_type=F32)

    shift0 = CONV_HALO - (CONV_WIDTH - 1)
    R, CL = CONV_ROW_CHUNK, CONV_LANE_CHUNK
    for r0 in range(0, rows, MIX_MATMUL_ROWS):
        for rc in range(r0, r0 + MIX_MATMUL_ROWS, R):
            for c0 in range(0, conv_ch, CL):
                y = None
                for b in range(SUBLANES):
                    z_rows = R if b == 0 else R + SUBLANES
                    z = None
                    for a in range((shift0 + CONV_WIDTH - 1) // SUBLANES + 1):
                        w = SUBLANES * a + b - shift0
                        if 0 <= w < CONV_WIDTH:
                            lo = rc + SUBLANES * a
                            tap = ext_ref[lo:lo + z_rows, c0:c0 + CL] * wdw_ref[w:w + 1, c0:c0 + CL]
                            z = tap if z is None else z + tap
                    part = z[b:b + R]
                    y = part if y is None else y + part
                y_ref[rc:rc + R, c0:c0 + CL] = y + bdw_ref[:, c0:c0 + CL]

        for rl in range(r0, r0 + MIX_MATMUL_ROWS, LN_ROW_CHUNK):
            y = y_ref[rl:rl + LN_ROW_CHUNK, :]
            mu = jnp.mean(y, axis=-1, keepdims=True)
            var = jnp.mean(jnp.square(y - mu), axis=-1, keepdims=True)
            y = (y - mu) * lax.rsqrt(var + LN_EPS) * lng_ref[...] + lnb_ref[...]
            y = y * (1.0 / (1.0 + jnp.exp(-y)))
            conv_ref[rl:rl + LN_ROW_CHUNK, :] = y.astype(BF16)

        conv = conv_ref[r0:r0 + MIX_MATMUL_ROWS, :]
        cols = []
        for c in range(0, d_model, MIX_COLS):
            r = jnp.dot(conv, wo_ref[attn_w:attn_w + conv_ch, c:c + MIX_COLS], preferred_element_type=F32)
            cols.append(mixed_ref[r0:r0 + MIX_MATMUL_ROWS, c:c + MIX_COLS] + r)
        mixed = jnp.concatenate(cols, axis=-1)
        ms = jnp.mean(mixed * mixed, axis=-1, keepdims=True)
        h1 = x_ref[0, r0:r0 + MIX_MATMUL_ROWS, :] + mixed * lax.rsqrt(ms + RMS_EPS) * gpost_ref[...]
        h1_ref[0, r0:r0 + MIX_MATMUL_ROWS, :] = h1
        ms1 = jnp.mean(h1 * h1, axis=-1, keepdims=True)
        f_ref[0, r0:r0 + MIX_MATMUL_ROWS, :] = (h1 * lax.rsqrt(ms1 + RMS_EPS) * gpre_ref[...]).astype(BF16)


def _mix(x, attn, hglu, w_dw, b_dw, ln_g, ln_b, w_out, g_post, g_pre, w_ff1, w_ff2):
    B, S, D = x.shape
    conv_ch = hglu.shape[2]
    rows = MIX_ROWS
    halo_per_tile = rows // CONV_HALO
    tiles_per_seq = S // rows
    d_ff = w_ff1.shape[1]
    ff_slice = d_ff // (B * tiles_per_seq)
    assert ff_slice * B * tiles_per_seq == d_ff and ff_slice % LANES == 0
    return pl.pallas_call(
        _mix_kernel,
        grid=(B, tiles_per_seq),
        in_specs=[
            pl.BlockSpec((1, rows, D), lambda b, i: (b, i, 0)),
            pl.BlockSpec((1, rows, attn.shape[2]), lambda b, i: (b, i, 0)),
            pl.BlockSpec((1, rows, conv_ch), lambda b, i: (b, i, 0)),
            pl.BlockSpec((1, CONV_HALO, conv_ch),
                         lambda b, i: (b, jnp.maximum(i * halo_per_tile - 1, 0), 0)),
            _resident(w_dw.shape),
            _resident((1, conv_ch)),
            _resident((1, conv_ch)),
            _resident((1, conv_ch)),
            _resident(w_out.shape),
            _resident((1, D)),
            _resident((1, D)),
            pl.BlockSpec((D, ff_slice), lambda b, i: (0, b * tiles_per_seq + i)),
            pl.BlockSpec((ff_slice, D), lambda b, i: (b * tiles_per_seq + i, 0)),
        ],
        out_specs=(
            pl.BlockSpec((1, rows, D), lambda b, i: (b, i, 0)),
            pl.BlockSpec((1, rows, D), lambda b, i: (b, i, 0)),
            pl.BlockSpec((D, ff_slice), lambda b, i: (0, b * tiles_per_seq + i)),
            pl.BlockSpec((ff_slice, D), lambda b, i: (b * tiles_per_seq + i, 0)),
        ),
        out_shape=(jax.ShapeDtypeStruct((B, S, D), F32), jax.ShapeDtypeStruct((B, S, D), BF16),
                   jax.ShapeDtypeStruct(w_ff1.shape, BF16), jax.ShapeDtypeStruct(w_ff2.shape, BF16)),
        scratch_shapes=[
            pltpu.VMEM((CONV_HALO + rows, conv_ch), F32),
            pltpu.VMEM((rows, conv_ch), F32),
            pltpu.VMEM((rows, conv_ch), BF16),
            pltpu.VMEM((rows, D), F32),
        ],
        compiler_params=pltpu.CompilerParams(
            dimension_semantics=("arbitrary", "arbitrary"),
            vmem_limit_bytes=V7X_VMEM_LIMIT_BYTES),
        name="mix",
    )(x, attn, hglu, hglu, w_dw, b_dw, ln_g, ln_b, w_out, g_post, g_pre, w_ff1, w_ff2)


def _ffn_kernel(f_ref, h1_ref, w1_ref, w2_ref, g_ref, o_ref, acc_ref):
    kf = pl.program_id(2)

    @pl.when(kf == 0)
    def _():
        acc_ref[...] = jnp.zeros_like(acc_ref)

    u = jnp.dot(f_ref[0], w1_ref[...], preferred_element_type=F32)
    u = jnp.square(jnp.maximum(u, 0.0)).astype(BF16)
    acc_ref[...] += jnp.dot(u, w2_ref[...], preferred_element_type=F32)

    @pl.when(kf == pl.num_programs(2) - 1)
    def _():
        a = acc_ref[...]
        ms = jnp.mean(a * a, axis=-1, keepdims=True)
        o_ref[0] = h1_ref[0] + a * lax.rsqrt(ms + RMS_EPS) * g_ref[...]


def _ffn(f, h1, w1, w2, g):
    B, S, D = h1.shape
    d_ff = w1.shape[1]
    rows, cols = FFN_ROWS, FFN_COLS
    return pl.pallas_call(
        _ffn_kernel,
        grid=(B, S // rows, d_ff // cols),
        in_specs=[
            pl.BlockSpec((1, rows, D), lambda b, i, k: (b, i, 0)),
            pl.BlockSpec((1, rows, D), lambda b, i, k: (b, i, 0)),
            pl.BlockSpec((D, cols), lambda b, i, k: (0, k)),
            pl.BlockSpec((cols, D), lambda b, i, k: (k, 0)),
            _resident((1, D)),
        ],
        out_specs=pl.BlockSpec((1, rows, D), lambda b, i, k: (b, i, 0)),
        out_shape=jax.ShapeDtypeStruct((B, S, D), F32),
        scratch_shapes=[pltpu.VMEM((rows, D), F32)],
        compiler_params=pltpu.CompilerParams(
            dimension_semantics=("arbitrary", "arbitrary", "arbitrary"),
            vmem_limit_bytes=V7X_VMEM_LIMIT_BYTES),
        name="ffn",
    )(f, h1, w1, w2, g)


def kernel(x, g_mix_pre, w_in, b_glu, w_dw, b_dw, ln_conv_g, ln_conv_b, w_out, g_mix_post,
           g_ffn_pre, w_ff1, w_ff2, g_ffn_post):
    B, S, D = x.shape
    depth = w_in.shape[0]
    assert S % IN_PROJ_ROWS == 0 and S % MIX_ROWS == 0 and S % FFN_ROWS == 0
    assert IN_PROJ_ROWS % MOBA_BLOCK == 0 and MIX_ROWS % CONV_HALO == 0
    assert MIX_ROWS % MIX_MATMUL_ROWS == 0 and MIX_MATMUL_ROWS % CONV_ROW_CHUNK == 0
    assert CONV_HALO >= CONV_WIDTH - 1 and CONV_HALO % SUBLANES == 0
    row = lambda v: v.reshape(1, -1)

    h = x
    for l in range(depth):
        k, qvt, kmean, hglu, w_out_b = _in_proj(h, row(g_mix_pre[l]), w_in[l].astype(BF16), row(b_glu[l]),
                                                w_out[l])
        kmean = kmean.reshape(B, S // MOBA_BLOCK, ATTN_WIDTH)
        attn = _moba(k, qvt, kmean)
        h1, f, w1b, w2b = _mix(h, attn, hglu, w_dw[l], row(b_dw[l]), row(ln_conv_g[l]), row(ln_conv_b[l]),
                               w_out_b, row(g_mix_post[l]), row(g_ffn_pre[l]), w_ff1[l], w_ff2[l])
        h = _ffn(f, h1, w1b, w2b, row(g_ffn_post[l]))
    return h
```

```python
import functools

import numpy as np
import jax
import jax.numpy as jnp
from jax import lax
from jax.experimental import pallas as pl
from jax.experimental.pallas import tpu as pltpu

F32 = jnp.float32
BF16 = jnp.bfloat16

HEAD_DIM = 128
N_HEADS = 8
ATTN_WIDTH = N_HEADS * HEAD_DIM
MOBA_BLOCK = 256
MOBA_TOPK = 3
CONV_WIDTH = 31
RMS_EPS = 1e-6
LN_EPS = 1e-5

V7X_VMEM_LIMIT_BYTES = 60000 * 1024
SUBLANES = 8
LANES = 128

IN_PROJ_ROWS = 512
IN_PROJ_COLS = 512
MIX_ROWS = 512
MIX_MATMUL_ROWS = 256
MIX_COLS = 512
CONV_HALO = 32
CONV_ROW_CHUNK = 128
CONV_LANE_CHUNK = 128
LN_ROW_CHUNK = 64
ATTN_UNROLL = 2
ATTN_KEY_GROUP = 2
ATTN_INIT_TILES = 4
ATTN_BODY_STEPS = 4
FFN_ROWS = 512
FFN_COLS = 1024

LOG2E = float(np.log2(np.e))
Q_SCALE = HEAD_DIM ** -0.5 * LOG2E
ALIBI_PARTS = 3
V_PAD_ROWS = 16
MASKED = -1e30
NT_DIMS = (((1,), (1,)), ((), ()))


def _resident(shape):
    return pl.BlockSpec(shape, lambda *_: (0,) * len(shape), pipeline_mode=pl.Buffered(1))


def _alibi_slopes(n_heads):
    return (2.0 ** (-8.0 * np.arange(1, n_heads + 1) / n_heads)).astype(np.float32)


def _in_proj_kernel(x_ref, g_ref, w_ref, bglu_ref, wo_ref,
                    k_ref, qvt_ref, kmean_ref, h_ref, wob_ref, wqvt_ref, *, scale):
    rows = x_ref.shape[1]
    conv_ch = h_ref.shape[2]
    k_col, v_col, glu_col = ATTN_WIDTH, 2 * ATTN_WIDTH, 3 * ATTN_WIDTH

    wob_ref[...] = wo_ref[...].astype(BF16)

    @pl.when(jnp.logical_and(pl.program_id(0) == 0, pl.program_id(1) == 0))
    def _():
        for rc in range(0, 2 * ATTN_WIDTH, MOBA_BLOCK):
            col = rc if rc < ATTN_WIDTH else v_col + rc - ATTN_WIDTH
            wqvt_ref[rc:rc + MOBA_BLOCK, :] = w_ref[:, col:col + MOBA_BLOCK].astype(F32).T.astype(BF16)

    x = x_ref[0]
    ms = jnp.mean(x * x, axis=-1, keepdims=True)
    a = (x * lax.rsqrt(ms + RMS_EPS) * g_ref[...]).astype(BF16)

    for c in range(0, ATTN_WIDTH, IN_PROJ_COLS):
        r = jnp.dot(a, w_ref[:, k_col + c:k_col + c + IN_PROJ_COLS], preferred_element_type=F32)
        blocks = r.reshape(rows // MOBA_BLOCK, MOBA_BLOCK, IN_PROJ_COLS)
        kmean_ref[0, 0, :, c:c + IN_PROJ_COLS] = jnp.mean(blocks, axis=1)
        k_ref[0, :, c:c + IN_PROJ_COLS] = r.astype(BF16)

    for rc in range(0, 2 * ATTN_WIDTH, MOBA_BLOCK):
        r = lax.dot_general(wqvt_ref[rc:rc + MOBA_BLOCK, :], a, NT_DIMS,
                            preferred_element_type=F32)
        if rc < ATTN_WIDTH:
            r = r * scale
        r = r.astype(BF16)
        for blk in range(rows // MOBA_BLOCK):
            qvt_ref[0, blk, rc:rc + MOBA_BLOCK, :] = r[:, blk * MOBA_BLOCK:(blk + 1) * MOBA_BLOCK]

    for c in range(0, conv_ch, IN_PROJ_COLS):
        val = jnp.dot(a, w_ref[:, glu_col + c:glu_col + c + IN_PROJ_COLS], preferred_element_type=F32)
        val = val + bglu_ref[:, c:c + IN_PROJ_COLS]
        gate_col = glu_col + conv_ch + c
        gt = jnp.dot(a, w_ref[:, gate_col:gate_col + IN_PROJ_COLS], preferred_element_type=F32)
        gt = gt + bglu_ref[:, conv_ch + c:conv_ch + c + IN_PROJ_COLS]
        h_ref[0, :, c:c + IN_PROJ_COLS] = val * (1.0 / (1.0 + jnp.exp(-gt)))


def _in_proj(x, g, w, b_glu, w_out):
    B, S, D = x.shape
    conv_ch = b_glu.shape[1] // 2
    assert w.shape == (D, 3 * ATTN_WIDTH + 2 * conv_ch)
    rows = IN_PROJ_ROWS
    n_blk = rows // MOBA_BLOCK
    tiles_per_seq = S // rows
    grid = (B, tiles_per_seq)
    wo_slice = w_out.shape[0] // (B * tiles_per_seq)
    assert wo_slice * B * tiles_per_seq == w_out.shape[0] and wo_slice % (2 * SUBLANES) == 0
    out_shape = (
        jax.ShapeDtypeStruct((B, S, ATTN_WIDTH), BF16),
        jax.ShapeDtypeStruct((B, S // MOBA_BLOCK, 2 * ATTN_WIDTH, MOBA_BLOCK), BF16),
        jax.ShapeDtypeStruct((B, S // rows, n_blk, ATTN_WIDTH), F32),
        jax.ShapeDtypeStruct((B, S, conv_ch), F32),
        jax.ShapeDtypeStruct(w_out.shape, BF16),
    )
    return pl.pallas_call(
        functools.partial(_in_proj_kernel, scale=Q_SCALE),
        grid=grid,
        in_specs=[
            pl.BlockSpec((1, rows, D), lambda b, i: (b, i, 0)),
            _resident((1, D)),
            _resident(w.shape),
            _resident((1, 2 * conv_ch)),
            pl.BlockSpec((wo_slice, w_out.shape[1]), lambda b, i: (b * tiles_per_seq + i, 0)),
        ],
        out_specs=(
            pl.BlockSpec((1, rows, ATTN_WIDTH), lambda b, i: (b, i, 0)),
            pl.BlockSpec((1, n_blk, 2 * ATTN_WIDTH, MOBA_BLOCK), lambda b, i: (b, i, 0, 0)),
            pl.BlockSpec((1, 1, n_blk, ATTN_WIDTH), lambda b, i: (b, i, 0, 0)),
            pl.BlockSpec((1, rows, conv_ch), lambda b, i: (b, i, 0)),
            pl.BlockSpec((wo_slice, w_out.shape[1]), lambda b, i: (b * tiles_per_seq + i, 0)),
        ),
        out_shape=out_shape,
        scratch_shapes=[pltpu.VMEM((2 * ATTN_WIDTH, D), BF16)],
        compiler_params=pltpu.CompilerParams(
            dimension_semantics=("arbitrary", "arbitrary"),
            vmem_limit_bytes=V7X_VMEM_LIMIT_BYTES),
        name="in_proj",
    )(x, g, w, b_glu, w_out)


def _pair_schedule(n_blocks, unroll, group):
    todo = {}
    for qi in range(1, n_blocks):
        blocks = list(range(qi)) + [-1] * (-qi % group)
        todo[qi] = [blocks[i:i + group] for i in range(0, len(blocks), group)]
    key_blocks, query_tiles = [], []
    while any(todo.values()):
        tiles = sorted((qi for qi in todo if todo[qi]), key=lambda t: -len(todo[t]))[:unroll]
        for qi in tiles:
            key_blocks += todo[qi].pop(0)
            query_tiles.append(qi)
        for _ in range(unroll - len(tiles)):
            key_blocks += [-1] * group
            query_tiles.append(n_blocks)
    while (len(query_tiles) // unroll) % ATTN_BODY_STEPS:
        key_blocks += [-1] * group * unroll
        query_tiles += [n_blocks] * unroll
    return np.asarray(key_blocks, np.int32), np.asarray(query_tiles, np.int32)


def _moba_kernel(slopes_ref, pair_k_ref, pair_q_ref, qext_ref, qt_ref, k_ref, vt_ref, kmean_ref, o_ref,
                 qaug_ref, kaug_ref, vaug_ref, pen_ref, m_ref, acc_ref,
                 s_even_ref, s_odd_ref, smax_even_ref, smax_odd_ref):
    L = MOBA_BLOCK
    U = ATTN_UNROLL
    G = ATTN_KEY_GROUP
    T = ATTN_INIT_TILES
    Dh = HEAD_DIM
    n_blocks = vt_ref.shape[1]
    n_steps = pair_q_ref.shape[0] // U
    slope = slopes_ref[pl.program_id(1)]

    lane = lax.broadcasted_iota(jnp.int32, (L, Dh), 1)
    key_cols = jnp.where(lane < ALIBI_PARTS, lax.broadcasted_iota(jnp.int32, (L, Dh), 0), 0).astype(BF16)
    ones_row = (lax.broadcasted_iota(jnp.int32, (V_PAD_ROWS, L), 0) == 0).astype(BF16)
    for blk in range(n_blocks):
        kaug_ref[blk * L:(blk + 1) * L, 0:Dh] = k_ref[0, blk * L:(blk + 1) * L, :]
        kaug_ref[blk * L:(blk + 1) * L, Dh:2 * Dh] = key_cols
        qaug_ref[blk, 0:Dh, :] = qt_ref[0, blk]
        qaug_ref[blk, Dh:2 * Dh, :] = qext_ref[0]
        vaug_ref[blk, 0:Dh, :] = vt_ref[0, blk]
        vaug_ref[blk, Dh:Dh + V_PAD_ROWS, :] = ones_row

    key_local = lax.broadcasted_iota(jnp.int32, (L, L), 0)
    qry_local = lax.broadcasted_iota(jnp.int32, (L, L), 1)
    kmean = kmean_ref[0].astype(BF16)
    blk_id = lax.broadcasted_iota(jnp.int32, (n_blocks, L), 0).astype(F32)

    def init_tiles(g, carry):
        tiles = [g * T + u for u in range(T)]
        for u, qi in enumerate(tiles):
            k_own = kaug_ref[pl.ds(pl.multiple_of(qi * L, L), L), :]
            s_even_ref[u] = jnp.dot(k_own, qaug_ref[qi], preferred_element_type=F32)
        gates = [jnp.dot(kmean, qt_ref[0, qi], preferred_element_type=F32) for qi in tiles]
        for u, qi in enumerate(tiles):
            gate = jnp.where(blk_id < jnp.asarray(qi, F32), gates[u], -jnp.inf)
            sel = jnp.zeros((n_blocks, L), F32)
            for r in range(MOBA_TOPK):
                best = jnp.max(gate, axis=0, keepdims=True)
                first = jnp.min(jnp.where(gate == best, blk_id, float(n_blocks)), axis=0, keepdims=True)
                hit = blk_id == first
                sel = jnp.maximum(sel, jnp.where(hit, jnp.asarray(qi > r, F32), 0.0))
                gate = jnp.where(hit, -jnp.inf, gate)
            pen_ref[qi] = jnp.where(sel > 0.0, 0.0, MASKED)

            s = jnp.where(key_local <= qry_local, s_even_ref[u], -jnp.inf)
            m = jnp.max(s, axis=0, keepdims=True)
            p = jnp.exp2(s - m)
            m_ref[qi] = m
            acc_ref[qi] = jnp.dot(vaug_ref[qi], p.astype(BF16), preferred_element_type=F32)
        return carry

    lax.fori_loop(0, n_blocks // T, init_tiles, 0)
    pen_ref[n_blocks] = jnp.full((n_blocks, L), MASKED, F32)
    m_ref[n_blocks] = jnp.zeros((1, L), F32)
    acc_ref[n_blocks] = jnp.zeros(acc_ref.shape[1:], F32)

    def key_block(step, u, t):
        j = pair_k_ref[(step * U + u) * G + t]
        return jnp.maximum(j, 0), j >= 0

    def scores(step, s_ref, smax_ref):
        for u in range(U):
            qi = jnp.minimum(pair_q_ref[step * U + u], n_blocks - 1)
            for t in range(G):
                j, _ = key_block(step, u, t)
                k_blk = kaug_ref[pl.ds(pl.multiple_of(j * L, L), L), :]
                s = jnp.dot(k_blk, qaug_ref[qi], preferred_element_type=F32)
                s_ref[u * G + t] = s
                smax_ref[u * G + t] = jnp.max(s, axis=0, keepdims=True)

    def update(step, s_ref, smax_ref):
        loaded = []
        for u in range(U):
            slot = pair_q_ref[step * U + u]
            blocks = []
            for t in range(G):
                j, present = key_block(step, u, t)
                row = pen_ref[slot, pl.ds(j, 1), :] - slope * float(L) * jnp.asarray(slot - j, F32)
                blocks.append((j, jnp.where(present, row, MASKED)))
            loaded.append((u, slot, blocks, m_ref[slot]))
        updated = []
        for u, slot, blocks, m in loaded:
            m_new = m
            for t, (j, row) in enumerate(blocks):
                m_new = jnp.maximum(m_new, smax_ref[u * G + t] + row)
            alpha = jnp.exp2(m - m_new)
            pv = None
            for t, (j, row) in enumerate(blocks):
                p = jnp.exp2(s_ref[u * G + t] + (row - m_new))
                part = jnp.dot(vaug_ref[j], p.astype(BF16), preferred_element_type=F32)
                pv = part if pv is None else pv + part
            acc_ref[slot] = alpha * acc_ref[slot] + pv
            updated.append((slot, m_new))
        for slot, m in updated:
            m_ref[slot] = m

    def several_steps(t, carry):
        for i in range(ATTN_BODY_STEPS):
            step = ATTN_BODY_STEPS * t + i
            cur, nxt = (even, odd) if i % 2 == 0 else (odd, even)
            scores(jnp.minimum(step + 1, n_steps - 1), *nxt)
            update(step, *cur)
        return carry

    even, odd = (s_even_ref, smax_even_ref), (s_odd_ref, smax_odd_ref)
    scores(0, *even)
    lax.fori_loop(0, n_steps // ATTN_BODY_STEPS, several_steps, 0)

    def finish_tiles(g, carry):
        for u in range(T):
            qi = g * T + u
            q0 = pl.multiple_of(qi * L, L)
            acc = acc_ref[qi]
            o_ref[0, pl.ds(q0, L), :] = (acc[0:Dh] / acc[Dh:Dh + 1]).T.astype(o_ref.dtype)
        return carry

    lax.fori_loop(0, n_blocks // T, finish_tiles, 0)


def _bf16_parts(x, n):
    parts, rest = [], np.asarray(x, np.float32)
    for _ in range(n):
        part = rest.astype(BF16).astype(np.float32)
        parts.append(part)
        rest = (rest - part).astype(np.float32)
    return np.stack(parts, axis=-1)


def _moba(k, qvt, kmean):
    B, S, _ = k.shape
    n_blocks = S // MOBA_BLOCK
    assert n_blocks % ATTN_INIT_TILES == 0 and ATTN_BODY_STEPS % 2 == 0
    assert ATTN_UNROLL * ATTN_KEY_GROUP >= ATTN_INIT_TILES
    pair_k, pair_q = _pair_schedule(n_blocks, ATTN_UNROLL, ATTN_KEY_GROUP)
    n_tiles = ATTN_UNROLL * ATTN_KEY_GROUP
    slopes = (_alibi_slopes(N_HEADS) * LOG2E).astype(np.float32)
    qext = np.zeros((N_HEADS, HEAD_DIM, MOBA_BLOCK), np.float32)
    qext[:, :ALIBI_PARTS, :] = _bf16_parts(slopes, ALIBI_PARTS)[:, :, None]
    smem = pl.BlockSpec(memory_space=pltpu.SMEM)
    return pl.pallas_call(
        _moba_kernel,
        grid=(B, N_HEADS),
        in_specs=[
            smem, smem, smem,
            pl.BlockSpec((1, HEAD_DIM, MOBA_BLOCK), lambda b, h: (h, 0, 0)),
            pl.BlockSpec((1, n_blocks, HEAD_DIM, MOBA_BLOCK), lambda b, h: (b, 0, h, 0)),
            pl.BlockSpec((1, S, HEAD_DIM), lambda b, h: (b, 0, h)),
            pl.BlockSpec((1, n_blocks, HEAD_DIM, MOBA_BLOCK), lambda b, h: (b, 0, N_HEADS + h, 0)),
            pl.BlockSpec((1, n_blocks, HEAD_DIM), lambda b, h: (b, 0, h)),
        ],
        out_specs=pl.BlockSpec((1, S, HEAD_DIM), lambda b, h: (b, 0, h)),
        out_shape=jax.ShapeDtypeStruct((B, S, ATTN_WIDTH), BF16),
        scratch_shapes=[
            pltpu.VMEM((n_blocks, 2 * HEAD_DIM, MOBA_BLOCK), BF16),
            pltpu.VMEM((S, 2 * HEAD_DIM), BF16),
            pltpu.VMEM((n_blocks, HEAD_DIM + V_PAD_ROWS, MOBA_BLOCK), BF16),
            pltpu.VMEM((n_blocks + 1, n_blocks, MOBA_BLOCK), F32),
            pltpu.VMEM((n_blocks + 1, 1, MOBA_BLOCK), F32),
            pltpu.VMEM((n_blocks + 1, HEAD_DIM + V_PAD_ROWS, MOBA_BLOCK), F32),
            pltpu.VMEM((n_tiles, MOBA_BLOCK, MOBA_BLOCK), F32),
            pltpu.VMEM((n_tiles, MOBA_BLOCK, MOBA_BLOCK), F32),
            pltpu.VMEM((n_tiles, 1, MOBA_BLOCK), F32),
            pltpu.VMEM((n_tiles, 1, MOBA_BLOCK), F32),
        ],
        compiler_params=pltpu.CompilerParams(
            dimension_semantics=("arbitrary", "arbitrary"),
            vmem_limit_bytes=V7X_VMEM_LIMIT_BYTES),
        name="moba",
    )(jnp.asarray(slopes), jnp.asarray(pair_k), jnp.asarray(pair_q), jnp.asarray(qext, BF16),
      qvt, k, qvt, kmean)


def _mix_kernel(x_ref, attn_ref, hmain_ref, hhalo_ref, wdw_ref, bdw_ref, lng_ref, lnb_ref,
                wo_ref, gpost_ref, gpre_ref, w1_ref, w2_ref,
                h1_ref, f_ref, w1b_ref, w2b_ref, ext_ref, y_ref, conv_ref, mixed_ref):
    rows = x_ref.shape[1]
    conv_ch = hmain_ref.shape[2]
    first_tile = pl.program_id(1) == 0

    w1b_ref[...] = w1_ref[...].astype(BF16)
    w2b_ref[...] = w2_ref[...].astype(BF16)

    ext_ref[0:CONV_HALO, :] = jnp.where(first_tile, 0.0, hhalo_ref[0])
    ext_ref[CONV_HALO:CONV_HALO + rows, :] = hmain_ref[0]

    attn = attn_ref[0]
    attn_w = attn.shape[1]
    d_model = wo_ref.shape[1]
    for c in range(0, d_model, MIX_COLS):
        mixed_ref[:, c:c + MIX_COLS] = jnp.dot(attn, wo_ref[0:attn_w, c:c + MIX_COLS],
                                               preferred_element_type=F32)

    shift0 = CONV_HALO - (CONV_WIDTH - 1)
    R, CL = CONV_ROW_CHUNK, CONV_LANE_CHUNK
    for r0 in range(0, rows, MIX_MATMUL_ROWS):
        for rc in range(r0, r0 + MIX_MATMUL_ROWS, R):
            for c0 in range(0, conv_ch, CL):
                y = None
                for b in range(SUBLANES):
                    z_rows = R if b == 0 else R + SUBLANES
                    z = None
                    for a in range((shift0 + CONV_WIDTH - 1) // SUBLANES + 1):
                        w = SUBLANES * a + b - shift0
                        if 0 <= w < CONV_WIDTH:
                            lo = rc + SUBLANES * a
                            tap = ext_ref[lo:lo + z_rows, c0:c0 + CL] * wdw_ref[w:w + 1, c0:c0 + CL]
                            z = tap if z is None else z + tap
                    part = z[b:b + R]
                    y = part if y is None else y + part
                y_ref[rc:rc + R, c0:c0 + CL] = y + bdw_ref[:, c0:c0 + CL]

        for rl in range(r0, r0 + MIX_MATMUL_ROWS, LN_ROW_CHUNK):
            y = y_ref[rl:rl + LN_ROW_CHUNK, :]
            mu = jnp.mean(y, axis=-1, keepdims=True)
            var = jnp.mean(jnp.square(y - mu), axis=-1, keepdims=True)
            y = (y - mu) * lax.rsqrt(var + LN_EPS) * lng_ref[...] + lnb_ref[...]
            y = y * (1.0 / (1.0 + jnp.exp(-y)))
            conv_ref[rl:rl + LN_ROW_CHUNK, :] = y.astype(BF16)

        conv = conv_ref[r0:r0 + MIX_MATMUL_ROWS, :]
        cols = []
        for c in range(0, d_model, MIX_COLS):
            r = jnp.dot(conv, wo_ref[attn_w:attn_w + conv_ch, c:c + MIX_COLS], preferred_element_type=F32)
            cols.append(mixed_ref[r0:r0 + MIX_MATMUL_ROWS, c:c + MIX_COLS] + r)
        mixed = jnp.concatenate(cols, axis=-1)
        ms = jnp.mean(mixed * mixed, axis=-1, keepdims=True)
        h1 = x_ref[0, r0:r0 + MIX_MATMUL_ROWS, :] + mixed * lax.rsqrt(ms + RMS_EPS) * gpost_ref[...]
        h1_ref[0, r0:r0 + MIX_MATMUL_ROWS, :] = h1
        ms1 = jnp.mean(h1 * h1, axis=-1, keepdims=True)
        f_ref[0, r0:r0 + MIX_MATMUL_ROWS, :] = (h1 * lax.rsqrt(ms1 + RMS_EPS) * gpre_ref[...]).astype(BF16)


def _mix(x, attn, hglu, w_dw, b_dw, ln_g, ln_b, w_out, g_post, g_pre, w_ff1, w_ff2):
    B, S, D = x.shape
    conv_ch = hglu.shape[2]
    rows = MIX_ROWS
    halo_per_tile = rows // CONV_HALO
    tiles_per_seq = S // rows
    d_ff = w_ff1.shape[1]
    ff_slice = d_ff // (B * tiles_per_seq)
    assert ff_slice * B * tiles_per_seq == d_ff and ff_slice % LANES == 0
    return pl.pallas_call(
        _mix_kernel,
        grid=(B, tiles_per_seq),
        in_specs=[
            pl.BlockSpec((1, rows, D), lambda b, i: (b, i, 0)),
            pl.BlockSpec((1, rows, attn.shape[2]), lambda b, i: (b, i, 0)),
            pl.BlockSpec((1, rows, conv_ch), lambda b, i: (b, i, 0)),
            pl.BlockSpec((1, CONV_HALO, conv_ch),
                         lambda b, i: (b, jnp.maximum(i * halo_per_tile - 1, 0), 0)),
            _resident(w_dw.shape),
            _resident((1, conv_ch)),
            _resident((1, conv_ch)),
            _resident((1, conv_ch)),
            _resident(w_out.shape),
            _resident((1, D)),
            _resident((1, D)),
            pl.BlockSpec((D, ff_slice), lambda b, i: (0, b * tiles_per_seq + i)),
            pl.BlockSpec((ff_slice, D), lambda b, i: (b * tiles_per_seq + i, 0)),
        ],
        out_specs=(
            pl.BlockSpec((1, rows, D), lambda b, i: (b, i, 0)),
            pl.BlockSpec((1, rows, D), lambda b, i: (b, i, 0)),
            pl.BlockSpec((D, ff_slice), lambda b, i: (0, b * tiles_per_seq + i)),
            pl.BlockSpec((ff_slice, D), lambda b, i: (b * tiles_per_seq + i, 0)),
        ),
        out_shape=(jax.ShapeDtypeStruct((B, S, D), F32), jax.ShapeDtypeStruct((B, S, D), BF16),
                   jax.ShapeDtypeStruct(w_ff1.shape, BF16), jax.ShapeDtypeStruct(w_ff2.shape, BF16)),
        scratch_shapes=[
            pltpu.VMEM((CONV_HALO + rows, conv_ch), F32),
            pltpu.VMEM((rows, conv_ch), F32),
            pltpu.VMEM((rows, conv_ch), BF16),
            pltpu.VMEM((rows, D), F32),
        ],
        compiler_params=pltpu.CompilerParams(
            dimension_semantics=("arbitrary", "arbitrary"),
            vmem_limit_bytes=V7X_VMEM_LIMIT_BYTES),
        name="mix",
    )(x, attn, hglu, hglu, w_dw, b_dw, ln_g, ln_b, w_out, g_post, g_pre, w_ff1, w_ff2)


def _ffn_kernel(f_ref, h1_ref, w1_ref, w2_ref, g_ref, o_ref, acc_ref):
    kf = pl.program_id(2)

    @pl.when(kf == 0)
    def _():
        acc_ref[...] = jnp.zeros_like(acc_ref)

    u = jnp.dot(f_ref[0], w1_ref[...], preferred_element_type=F32)
    u = jnp.square(jnp.maximum(u, 0.0)).astype(BF16)
    acc_ref[...] += jnp.dot(u, w2_ref[...], preferred_element_type=F32)

    @pl.when(kf == pl.num_programs(2) - 1)
    def _():
        a = acc_ref[...]
        ms = jnp.mean(a * a, axis=-1, keepdims=True)
        o_ref[0] = h1_ref[0] + a * lax.rsqrt(ms + RMS_EPS) * g_ref[...]


def _ffn(f, h1, w1, w2, g):
    B, S, D = h1.shape
    d_ff = w1.shape[1]
    rows, cols = FFN_ROWS, FFN_COLS
    return pl.pallas_call(
        _ffn_kernel,
        grid=(B, S // rows, d_ff // cols),
        in_specs=[
            pl.BlockSpec((1, rows, D), lambda b, i, k: (b, i, 0)),
            pl.BlockSpec((1, rows, D), lambda b, i, k: (b, i, 0)),
            pl.BlockSpec((D, cols), lambda b, i, k: (0, k)),
            pl.BlockSpec((cols, D), lambda b, i, k: (k, 0)),
            _resident((1, D)),
        ],
        out_specs=pl.BlockSpec((1, rows, D), lambda b, i, k: (b, i, 0)),
        out_shape=jax.ShapeDtypeStruct((B, S, D), F32),
        scratch_shapes=[pltpu.VMEM((rows, D), F32)],
        compiler_params=pltpu.CompilerParams(
            dimension_semantics=("arbitrary", "arbitrary", "arbitrary"),
            vmem_limit_bytes=V7X_VMEM_LIMIT_BYTES),
        name="ffn",
    )(f, h1, w1, w2, g)


def kernel(x, g_mix_pre, w_in, b_glu, w_dw, b_dw, ln_conv_g, ln_conv_b, w_out, g_mix_post,
           g_ffn_pre, w_ff1, w_ff2, g_ffn_post):
    B, S, D = x.shape
    depth = w_in.shape[0]
    assert S % IN_PROJ_ROWS == 0 and S % MIX_ROWS == 0 and S % FFN_ROWS == 0
    assert IN_PROJ_ROWS % MOBA_BLOCK == 0 and MIX_ROWS % CONV_HALO == 0
    assert MIX_ROWS % MIX_MATMUL_ROWS == 0 and MIX_MATMUL_ROWS % CONV_ROW_CHUNK == 0
    assert CONV_HALO >= CONV_WIDTH - 1 and CONV_HALO % SUBLANES == 0
    row = lambda v: v.reshape(1, -1)

    h = x
    for l in range(depth):
        k, qvt, kmean, hglu, w_out_b = _in_proj(h, row(g_mix_pre[l]), w_in[l].astype(BF16), row(b_glu[l]),
                                                w_out[l])
        kmean = kmean.reshape(B, S // MOBA_BLOCK, ATTN_WIDTH)
        attn = _moba(k, qvt, kmean)
        h1, f, w1b, w2b = _mix(h, attn, hglu, w_dw[l], row(b_dw[l]), row(ln_conv_g[l]), row(ln_conv_b[l]),
                               w_out_b, row(g_mix_post[l]), row(g_ffn_pre[l]), w_ff1[l], w_ff2[l])
        h = _ffn(f, h1, w1b, w2b, row(g_ffn_post[l]))
    return h
```

```python
import functools

import numpy as np
import jax
import jax.numpy as jnp
from jax import lax
from jax.experimental import pallas as pl
from jax.experimental.pallas import tpu as pltpu

F32 = jnp.float32
BF16 = jnp.bfloat16

HEAD_DIM = 128
N_HEADS = 8
ATTN_WIDTH = N_HEADS * HEAD_DIM
MOBA_BLOCK = 256
MOBA_TOPK = 3
CONV_WIDTH = 31
RMS_EPS = 1e-6
LN_EPS = 1e-5

V7X_VMEM_LIMIT_BYTES = 60000 * 1024
SUBLANES = 8
LANES = 128

IN_PROJ_ROWS = 512
IN_PROJ_COLS = 512
MIX_ROWS = 512
MIX_MATMUL_ROWS = 256
MIX_COLS = 512
CONV_HALO = 32
CONV_ROW_CHUNK = 128
CONV_LANE_CHUNK = 128
LN_ROW_CHUNK = 64
ATTN_UNROLL = 2
ATTN_KEY_GROUP = 2
ATTN_INIT_TILES = 4
ATTN_BODY_STEPS = 8
FFN_ROWS = 512
FFN_COLS = 1024

LOG2E = float(np.log2(np.e))
Q_SCALE = HEAD_DIM ** -0.5 * LOG2E
ALIBI_PARTS = 3
V_PAD_ROWS = 16
MASKED = -1e30
NT_DIMS = (((1,), (1,)), ((), ()))


def _resident(shape):
    return pl.BlockSpec(shape, lambda *_: (0,) * len(shape), pipeline_mode=pl.Buffered(1))


def _alibi_slopes(n_heads):
    return (2.0 ** (-8.0 * np.arange(1, n_heads + 1) / n_heads)).astype(np.float32)


def _in_proj_kernel(x_ref, g_ref, w_ref, bglu_ref, wo_ref,
                    k_ref, qvt_ref, kmean_ref, h_ref, wob_ref, wqvt_ref, *, scale):
    rows = x_ref.shape[1]
    conv_ch = h_ref.shape[2]
    k_col, v_col, glu_col = ATTN_WIDTH, 2 * ATTN_WIDTH, 3 * ATTN_WIDTH

    wob_ref[...] = wo_ref[...].astype(BF16)

    @pl.when(jnp.logical_and(pl.program_id(0) == 0, pl.program_id(1) == 0))
    def _():
        for rc in range(0, 2 * ATTN_WIDTH, MOBA_BLOCK):
            col = rc if rc < ATTN_WIDTH else v_col + rc - ATTN_WIDTH
            wqvt_ref[rc:rc + MOBA_BLOCK, :] = w_ref[:, col:col + MOBA_BLOCK].astype(F32).T.astype(BF16)

    x = x_ref[0]
    ms = jnp.mean(x * x, axis=-1, keepdims=True)
    a = (x * lax.rsqrt(ms + RMS_EPS) * g_ref[...]).astype(BF16)

    for c in range(0, ATTN_WIDTH, IN_PROJ_COLS):
        r = jnp.dot(a, w_ref[:, k_col + c:k_col + c + IN_PROJ_COLS], preferred_element_type=F32)
        blocks = r.reshape(rows // MOBA_BLOCK, MOBA_BLOCK, IN_PROJ_COLS)
        kmean_ref[0, 0, :, c:c + IN_PROJ_COLS] = jnp.mean(blocks, axis=1)
        k_ref[0, :, c:c + IN_PROJ_COLS] = r.astype(BF16)

    for rc in range(0, 2 * ATTN_WIDTH, MOBA_BLOCK):
        r = lax.dot_general(wqvt_ref[rc:rc + MOBA_BLOCK, :], a, NT_DIMS,
                            preferred_element_type=F32)
        if rc < ATTN_WIDTH:
            r = r * scale
        r = r.astype(BF16)
        for blk in range(rows // MOBA_BLOCK):
            qvt_ref[0, blk, rc:rc + MOBA_BLOCK, :] = r[:, blk * MOBA_BLOCK:(blk + 1) * MOBA_BLOCK]

    for c in range(0, conv_ch, IN_PROJ_COLS):
        val = jnp.dot(a, w_ref[:, glu_col + c:glu_col + c + IN_PROJ_COLS], preferred_element_type=F32)
        val = val + bglu_ref[:, c:c + IN_PROJ_COLS]
        gate_col = glu_col + conv_ch + c
        gt = jnp.dot(a, w_ref[:, gate_col:gate_col + IN_PROJ_COLS], preferred_element_type=F32)
        gt = gt + bglu_ref[:, conv_ch + c:conv_ch + c + IN_PROJ_COLS]
        h_ref[0, :, c:c + IN_PROJ_COLS] = val * (1.0 / (1.0 + jnp.exp(-gt)))


def _in_proj(x, g, w, b_glu, w_out):
    B, S, D = x.shape
    conv_ch = b_glu.shape[1] // 2
    assert w.shape == (D, 3 * ATTN_WIDTH + 2 * conv_ch)
    rows = IN_PROJ_ROWS
    n_blk = rows // MOBA_BLOCK
    tiles_per_seq = S // rows
    grid = (B, tiles_per_seq)
    wo_slice = w_out.shape[0] // (B * tiles_per_seq)
    assert wo_slice * B * tiles_per_seq == w_out.shape[0] and wo_slice % (2 * SUBLANES) == 0
    out_shape = (
        jax.ShapeDtypeStruct((B, S, ATTN_WIDTH), BF16),
        jax.ShapeDtypeStruct((B, S // MOBA_BLOCK, 2 * ATTN_WIDTH, MOBA_BLOCK), BF16),
        jax.ShapeDtypeStruct((B, S // rows, n_blk, ATTN_WIDTH), F32),
        jax.ShapeDtypeStruct((B, S, conv_ch), F32),
        jax.ShapeDtypeStruct(w_out.shape, BF16),
    )
    return pl.pallas_call(
        functools.partial(_in_proj_kernel, scale=Q_SCALE),
        grid=grid,
        in_specs=[
            pl.BlockSpec((1, rows, D), lambda b, i: (b, i, 0)),
            _resident((1, D)),
            _resident(w.shape),
            _resident((1, 2 * conv_ch)),
            pl.BlockSpec((wo_slice, w_out.shape[1]), lambda b, i: (b * tiles_per_seq + i, 0)),
        ],
        out_specs=(
            pl.BlockSpec((1, rows, ATTN_WIDTH), lambda b, i: (b, i, 0)),
            pl.BlockSpec((1, n_blk, 2 * ATTN_WIDTH, MOBA_BLOCK), lambda b, i: (b, i, 0, 0)),
            pl.BlockSpec((1, 1, n_blk, ATTN_WIDTH), lambda b, i: (b, i, 0, 0)),
            pl.BlockSpec((1, rows, conv_ch), lambda b, i: (b, i, 0)),
            pl.BlockSpec((wo_slice, w_out.shape[1]), lambda b, i: (b * tiles_per_seq + i, 0)),
        ),
        out_shape=out_shape,
        scratch_shapes=[pltpu.VMEM((2 * ATTN_WIDTH, D), BF16)],
        compiler_params=pltpu.CompilerParams(
            dimension_semantics=("arbitrary", "arbitrary"),
            vmem_limit_bytes=V7X_VMEM_LIMIT_BYTES),
        name="in_proj",
    )(x, g, w, b_glu, w_out)


def _pair_schedule(n_blocks, unroll, group):
    todo = {}
    for qi in range(1, n_blocks):
        blocks = list(range(qi)) + [-1] * (-qi % group)
        todo[qi] = [blocks[i:i + group] for i in range(0, len(blocks), group)]
    key_blocks, query_tiles = [], []
    while any(todo.values()):
        tiles = sorted((qi for qi in todo if todo[qi]), key=lambda t: -len(todo[t]))[:unroll]
        for qi in tiles:
            key_blocks += todo[qi].pop(0)
            query_tiles.append(qi)
        for _ in range(unroll - len(tiles)):
            key_blocks += [-1] * group
            query_tiles.append(n_blocks)
    while (len(query_tiles) // unroll) % ATTN_BODY_STEPS:
        key_blocks += [-1] * group * unroll
        query_tiles += [n_blocks] * unroll
    return np.asarray(key_blocks, np.int32), np.asarray(query_tiles, np.int32)


def _moba_kernel(slopes_ref, pair_k_ref, pair_q_ref, qext_ref, qt_ref, k_ref, vt_ref, kmean_ref, o_ref,
                 qaug_ref, kaug_ref, vaug_ref, pen_ref, m_ref, acc_ref,
                 s_even_ref, s_odd_ref, smax_even_ref, smax_odd_ref):
    L = MOBA_BLOCK
    U = ATTN_UNROLL
    G = ATTN_KEY_GROUP
    T = ATTN_INIT_TILES
    Dh = HEAD_DIM
    n_blocks = vt_ref.shape[1]
    n_steps = pair_q_ref.shape[0] // U
    slope = slopes_ref[pl.program_id(1)]

    lane = lax.broadcasted_iota(jnp.int32, (L, Dh), 1)
    key_cols = jnp.where(lane < ALIBI_PARTS, lax.broadcasted_iota(jnp.int32, (L, Dh), 0), 0).astype(BF16)
    ones_row = (lax.broadcasted_iota(jnp.int32, (V_PAD_ROWS, L), 0) == 0).astype(BF16)
    for blk in range(n_blocks):
        kaug_ref[blk * L:(blk + 1) * L, 0:Dh] = k_ref[0, blk * L:(blk + 1) * L, :]
        kaug_ref[blk * L:(blk + 1) * L, Dh:2 * Dh] = key_cols
        qaug_ref[blk, 0:Dh, :] = qt_ref[0, blk]
        qaug_ref[blk, Dh:2 * Dh, :] = qext_ref[0]
        vaug_ref[blk, 0:Dh, :] = vt_ref[0, blk]
        vaug_ref[blk, Dh:Dh + V_PAD_ROWS, :] = ones_row

    key_local = lax.broadcasted_iota(jnp.int32, (L, L), 0)
    qry_local = lax.broadcasted_iota(jnp.int32, (L, L), 1)
    kmean = kmean_ref[0].astype(BF16)
    blk_id = lax.broadcasted_iota(jnp.int32, (n_blocks, L), 0).astype(F32)

    def init_tiles(g, carry):
        tiles = [g * T + u for u in range(T)]
        for u, qi in enumerate(tiles):
            k_own = kaug_ref[pl.ds(pl.multiple_of(qi * L, L), L), :]
            s_even_ref[u] = jnp.dot(k_own, qaug_ref[qi], preferred_element_type=F32)
        gates = [jnp.dot(kmean, qt_ref[0, qi], preferred_element_type=F32) for qi in tiles]
        for u, qi in enumerate(tiles):
            gate = jnp.where(blk_id < jnp.asarray(qi, F32), gates[u], -jnp.inf)
            sel = jnp.zeros((n_blocks, L), F32)
            for r in range(MOBA_TOPK):
                best = jnp.max(gate, axis=0, keepdims=True)
                first = jnp.min(jnp.where(gate == best, blk_id, float(n_blocks)), axis=0, keepdims=True)
                hit = blk_id == first
                sel = jnp.maximum(sel, jnp.where(hit, jnp.asarray(qi > r, F32), 0.0))
                gate = jnp.where(hit, -jnp.inf, gate)
            pen_ref[qi] = jnp.where(sel > 0.0, 0.0, MASKED)

            s = jnp.where(key_local <= qry_local, s_even_ref[u], -jnp.inf)
            m = jnp.max(s, axis=0, keepdims=True)
            p = jnp.exp2(s - m)
            m_ref[qi] = m
            acc_ref[qi] = jnp.dot(vaug_ref[qi], p.astype(BF16), preferred_element_type=F32)
        return carry

    lax.fori_loop(0, n_blocks // T, init_tiles, 0)
    pen_ref[n_blocks] = jnp.full((n_blocks, L), MASKED, F32)
    m_ref[n_blocks] = jnp.zeros((1, L), F32)
    acc_ref[n_blocks] = jnp.zeros(acc_ref.shape[1:], F32)

    def key_block(step, u, t):
        j = pair_k_ref[(step * U + u) * G + t]
        return jnp.maximum(j, 0), j >= 0

    def scores(step, s_ref, smax_ref):
        for u in range(U):
            qi = jnp.minimum(pair_q_ref[step * U + u], n_blocks - 1)
            for t in range(G):
                j, _ = key_block(step, u, t)
                k_blk = kaug_ref[pl.ds(pl.multiple_of(j * L, L), L), :]
                s = jnp.dot(k_blk, qaug_ref[qi], preferred_element_type=F32)
                s_ref[u * G + t] = s
                smax_ref[u * G + t] = jnp.max(s, axis=0, keepdims=True)

    def update(step, s_ref, smax_ref):
        loaded = []
        for u in range(U):
            slot = pair_q_ref[step * U + u]
            blocks = []
            for t in range(G):
                j, present = key_block(step, u, t)
                row = pen_ref[slot, pl.ds(j, 1), :] - slope * float(L) * jnp.asarray(slot - j, F32)
                blocks.append((j, jnp.where(present, row, MASKED)))
            loaded.append((u, slot, blocks, m_ref[slot]))
        updated = []
        for u, slot, blocks, m in loaded:
            m_new = m
            for t, (j, row) in enumerate(blocks):
                m_new = jnp.maximum(m_new, smax_ref[u * G + t] + row)
            alpha = jnp.exp2(m - m_new)
            pv = None
            for t, (j, row) in enumerate(blocks):
                p = jnp.exp2(s_ref[u * G + t] + (row - m_new))
                part = jnp.dot(vaug_ref[j], p.astype(BF16), preferred_element_type=F32)
                pv = part if pv is None else pv + part
            acc_ref[slot] = alpha * acc_ref[slot] + pv
            updated.append((slot, m_new))
        for slot, m in updated:
            m_ref[slot] = m

    def several_steps(t, carry):
        for i in range(ATTN_BODY_STEPS):
            step = ATTN_BODY_STEPS * t + i
            cur, nxt = (even, odd) if i % 2 == 0 else (odd, even)
            scores(jnp.minimum(step + 1, n_steps - 1), *nxt)
            update(step, *cur)
        return carry

    even, odd = (s_even_ref, smax_even_ref), (s_odd_ref, smax_odd_ref)
    scores(0, *even)
    lax.fori_loop(0, n_steps // ATTN_BODY_STEPS, several_steps, 0)

    def finish_tiles(g, carry):
        for u in range(T):
            qi = g * T + u
            q0 = pl.multiple_of(qi * L, L)
            acc = acc_ref[qi]
            o_ref[0, pl.ds(q0, L), :] = (acc[0:Dh] / acc[Dh:Dh + 1]).T.astype(o_ref.dtype)
        return carry

    lax.fori_loop(0, n_blocks // T, finish_tiles, 0)


def _bf16_parts(x, n):
    parts, rest = [], np.asarray(x, np.float32)
    for _ in range(n):
        part = rest.astype(BF16).astype(np.float32)
        parts.append(part)
        rest = (rest - part).astype(np.float32)
    return np.stack(parts, axis=-1)


def _moba(k, qvt, kmean):
    B, S, _ = k.shape
    n_blocks = S // MOBA_BLOCK
    assert n_blocks % ATTN_INIT_TILES == 0 and ATTN_BODY_STEPS % 2 == 0
    assert ATTN_UNROLL * ATTN_KEY_GROUP >= ATTN_INIT_TILES
    pair_k, pair_q = _pair_schedule(n_blocks, ATTN_UNROLL, ATTN_KEY_GROUP)
    n_tiles = ATTN_UNROLL * ATTN_KEY_GROUP
    slopes = (_alibi_slopes(N_HEADS) * LOG2E).astype(np.float32)
    qext = np.zeros((N_HEADS, HEAD_DIM, MOBA_BLOCK), np.float32)
    qext[:, :ALIBI_PARTS, :] = _bf16_parts(slopes, ALIBI_PARTS)[:, :, None]
    smem = pl.BlockSpec(memory_space=pltpu.SMEM)
    return pl.pallas_call(
        _moba_kernel,
        grid=(B, N_HEADS),
        in_specs=[
            smem, smem, smem,
            pl.BlockSpec((1, HEAD_DIM, MOBA_BLOCK), lambda b, h: (h, 0, 0)),
            pl.BlockSpec((1, n_blocks, HEAD_DIM, MOBA_BLOCK), lambda b, h: (b, 0, h, 0)),
            pl.BlockSpec((1, S, HEAD_DIM), lambda b, h: (b, 0, h)),
            pl.BlockSpec((1, n_blocks, HEAD_DIM, MOBA_BLOCK), lambda b, h: (b, 0, N_HEADS + h, 0)),
            pl.BlockSpec((1, n_blocks, HEAD_DIM), lambda b, h: (b, 0, h)),
        ],
        out_specs=pl.BlockSpec((1, S, HEAD_DIM), lambda b, h: (b, 0, h)),
        out_shape=jax.ShapeDtypeStruct((B, S, ATTN_WIDTH), BF16),
        scratch_shapes=[
            pltpu.VMEM((n_blocks, 2 * HEAD_DIM, MOBA_BLOCK), BF16),
            pltpu.VMEM((S, 2 * HEAD_DIM), BF16),
            pltpu.VMEM((n_blocks, HEAD_DIM + V_PAD_ROWS, MOBA_BLOCK), BF16),
            pltpu.VMEM((n_blocks + 1, n_blocks, MOBA_BLOCK), F32),
            pltpu.VMEM((n_blocks + 1, 1, MOBA_BLOCK), F32),
            pltpu.VMEM((n_blocks + 1, HEAD_DIM + V_PAD_ROWS, MOBA_BLOCK), F32),
            pltpu.VMEM((n_tiles, MOBA_BLOCK, MOBA_BLOCK), F32),
            pltpu.VMEM((n_tiles, MOBA_BLOCK, MOBA_BLOCK), F32),
            pltpu.VMEM((n_tiles, 1, MOBA_BLOCK), F32),
            pltpu.VMEM((n_tiles, 1, MOBA_BLOCK), F32),
        ],
        compiler_params=pltpu.CompilerParams(
            dimension_semantics=("arbitrary", "arbitrary"),
            vmem_limit_bytes=V7X_VMEM_LIMIT_BYTES),
        name="moba",
    )(jnp.asarray(slopes), jnp.asarray(pair_k), jnp.asarray(pair_q), jnp.asarray(qext, BF16),
      qvt, k, qvt, kmean)


def _mix_kernel(x_ref, attn_ref, hmain_ref, hhalo_ref, wdw_ref, bdw_ref, lng_ref, lnb_ref,
                wo_ref, gpost_ref, gpre_ref, w1_ref, w2_ref,
                h1_ref, f_ref, w1b_ref, w2b_ref, ext_ref, y_ref, conv_ref, mixed_ref):
    rows = x_ref.shape[1]
    conv_ch = hmain_ref.shape[2]
    first_tile = pl.program_id(1) == 0

    w1b_ref[...] = w1_ref[...].astype(BF16)
    w2b_ref[...] = w2_ref[...].astype(BF16)

    ext_ref[0:CONV_HALO, :] = jnp.where(first_tile, 0.0, hhalo_ref[0])
    ext_ref[CONV_HALO:CONV_HALO + rows, :] = hmain_ref[0]

    attn = attn_ref[0]
    attn_w = attn.shape[1]
    d_model = wo_ref.shape[1]
    for c in range(0, d_model, MIX_COLS):
        mixed_ref[:, c:c + MIX_COLS] = jnp.dot(attn, wo_ref[0:attn_w, c:c + MIX_COLS],
                                               preferred_element_type=F32)

    shift0 = CONV_HALO - (CONV_WIDTH - 1)
    R, CL = CONV_ROW_CHUNK, CONV_LANE_CHUNK
    for r0 in range(0, rows, MIX_MATMUL_ROWS):
        for rc in range(r0, r0 + MIX_MATMUL_ROWS, R):
            for c0 in range(0, conv_ch, CL):
                y = None
                for b in range(SUBLANES):
                    z_rows = R if b == 0 else R + SUBLANES
                    z = None
                    for a in range((shift0 + CONV_WIDTH - 1) // SUBLANES + 1):
                        w = SUBLANES * a + b - shift0
                        if 0 <= w < CONV_WIDTH:
                            lo = rc + SUBLANES * a
                            tap = ext_ref[lo:lo + z_rows, c0:c0 + CL] * wdw_ref[w:w + 1, c0:c0 + CL]
                            z = tap if z is None else z + tap
                    part = z[b:b + R]
                    y = part if y is None else y + part
                y_ref[rc:rc + R, c0:c0 + CL] = y + bdw_ref[:, c0:c0 + CL]

        for rl in range(r0, r0 + MIX_MATMUL_ROWS, LN_ROW_CHUNK):
            y = y_ref[rl:rl + LN_ROW_CHUNK, :]
            mu = jnp.mean(y, axis=-1, keepdims=True)
            var = jnp.mean(jnp.square(y - mu), axis=-1, keepdims=True)
            y = (y - mu) * lax.rsqrt(var + LN_EPS) * lng_ref[...] + lnb_ref[...]
            y = y * (1.0 / (1.0 + jnp.exp(-y)))
            conv_ref[rl:rl + LN_ROW_CHUNK, :] = y.astype(BF16)

        conv = conv_ref[r0:r0 + MIX_MATMUL_ROWS, :]
        cols = []
        for c in range(0, d_model, MIX_COLS):
            r = jnp.dot(conv, wo_ref[attn_w:attn_w + conv_ch, c:c + MIX_COLS], preferred_element_type=F32)
            cols.append(mixed_ref[r0:r0 + MIX_MATMUL_ROWS, c:c + MIX_COLS] + r)
        mixed = jnp.concatenate(cols, axis=-1)
        ms = jnp.mean(mixed * mixed, axis=-1, keepdims=True)
        h1 = x_ref[0, r0:r0 + MIX_MATMUL_ROWS, :] + mixed * lax.rsqrt(ms + RMS_EPS) * gpost_ref[...]
        h1_ref[0, r0:r0 + MIX_MATMUL_ROWS, :] = h1
        ms1 = jnp.mean(h1 * h1, axis=-1, keepdims=True)
        f_ref[0, r0:r0 + MIX_MATMUL_ROWS, :] = (h1 * lax.rsqrt(ms1 + RMS_EPS) * gpre_ref[...]).astype(BF16)


def _mix(x, attn, hglu, w_dw, b_dw, ln_g, ln_b, w_out, g_post, g_pre, w_ff1, w_ff2):
    B, S, D = x.shape
    conv_ch = hglu.shape[2]
    rows = MIX_ROWS
    halo_per_tile = rows // CONV_HALO
    tiles_per_seq = S // rows
    d_ff = w_ff1.shape[1]
    ff_slice = d_ff // (B * tiles_per_seq)
    assert ff_slice * B * tiles_per_seq == d_ff and ff_slice % LANES == 0
    return pl.pallas_call(
        _mix_kernel,
        grid=(B, tiles_per_seq),
        in_specs=[
            pl.BlockSpec((1, rows, D), lambda b, i: (b, i, 0)),
            pl.BlockSpec((1, rows, attn.shape[2]), lambda b, i: (b, i, 0)),
            pl.BlockSpec((1, rows, conv_ch), lambda b, i: (b, i, 0)),
            pl.BlockSpec((1, CONV_HALO, conv_ch),
                         lambda b, i: (b, jnp.maximum(i * halo_per_tile - 1, 0), 0)),
            _resident(w_dw.shape),
            _resident((1, conv_ch)),
            _resident((1, conv_ch)),
            _resident((1, conv_ch)),
            _resident(w_out.shape),
            _resident((1, D)),
            _resident((1, D)),
            pl.BlockSpec((D, ff_slice), lambda b, i: (0, b * tiles_per_seq + i)),
            pl.BlockSpec((ff_slice, D), lambda b, i: (b * tiles_per_seq + i, 0)),
        ],
        out_specs=(
            pl.BlockSpec((1, rows, D), lambda b, i: (b, i, 0)),
            pl.BlockSpec((1, rows, D), lambda b, i: (b, i, 0)),
            pl.BlockSpec((D, ff_slice), lambda b, i: (0, b * tiles_per_seq + i)),
            pl.BlockSpec((ff_slice, D), lambda b, i: (b * tiles_per_seq + i, 0)),
        ),
        out_shape=(jax.ShapeDtypeStruct((B, S, D), F32), jax.ShapeDtypeStruct((B, S, D), BF16),
                   jax.ShapeDtypeStruct(w_ff1.shape, BF16), jax.ShapeDtypeStruct(w_ff2.shape, BF16)),
        scratch_shapes=[
            pltpu.VMEM((CONV_HALO + rows, conv_ch), F32),
            pltpu.VMEM((rows, conv_ch), F32),
            pltpu.VMEM((rows, conv_ch), BF16),
            pltpu.VMEM((rows, D), F32),
        ],
        compiler_params=pltpu.CompilerParams(
            dimension_semantics=("arbitrary", "arbitrary"),
            vmem_limit_bytes=V7X_VMEM_LIMIT_BYTES),
        name="mix",
    )(x, attn, hglu, hglu, w_dw, b_dw, ln_g, ln_b, w_out, g_post, g_pre, w_ff1, w_ff2)


def _ffn_kernel(f_ref, h1_ref, w1_ref, w2_ref, g_ref, o_ref, acc_ref):
    kf = pl.program_id(2)

    @pl.when(kf == 0)
    def _():
        acc_ref[...] = jnp.zeros_like(acc_ref)

    u = jnp.dot(f_ref[0], w1_ref[...], preferred_element_type=F32)
    u = jnp.square(jnp.maximum(u, 0.0)).astype(BF16)
    acc_ref[...] += jnp.dot(u, w2_ref[...], preferred_element_type=F32)

    @pl.when(kf == pl.num_programs(2) - 1)
    def _():
        a = acc_ref[...]
        ms = jnp.mean(a * a, axis=-1, keepdims=True)
        o_ref[0] = h1_ref[0] + a * lax.rsqrt(ms + RMS_EPS) * g_ref[...]


def _ffn(f, h1, w1, w2, g):
    B, S, D = h1.shape
    d_ff = w1.shape[1]
    rows, cols = FFN_ROWS, FFN_COLS
    return pl.pallas_call(
        _ffn_kernel,
        grid=(B, S // rows, d_ff // cols),
        in_specs=[
            pl.BlockSpec((1, rows, D), lambda b, i, k: (b, i, 0)),
            pl.BlockSpec((1, rows, D), lambda b, i, k: (b, i, 0)),
            pl.BlockSpec((D, cols), lambda b, i, k: (0, k)),
            pl.BlockSpec((cols, D), lambda b, i, k: (k, 0)),
            _resident((1, D)),
        ],
        out_specs=pl.BlockSpec((1, rows, D), lambda b, i, k: (b, i, 0)),
        out_shape=jax.ShapeDtypeStruct((B, S, D), F32),
        scratch_shapes=[pltpu.VMEM((rows, D), F32)],
        compiler_params=pltpu.CompilerParams(
            dimension_semantics=("arbitrary", "arbitrary", "arbitrary"),
            vmem_limit_bytes=V7X_VMEM_LIMIT_BYTES),
        name="ffn",
    )(f, h1, w1, w2, g)


def kernel(x, g_mix_pre, w_in, b_glu, w_dw, b_dw, ln_conv_g, ln_conv_b, w_out, g_mix_post,
           g_ffn_pre, w_ff1, w_ff2, g_ffn_post):
    B, S, D = x.shape
    depth = w_in.shape[0]
    assert S % IN_PROJ_ROWS == 0 and S % MIX_ROWS == 0 and S % FFN_ROWS == 0
    assert IN_PROJ_ROWS % MOBA_BLOCK == 0 and MIX_ROWS % CONV_HALO == 0
    assert MIX_ROWS % MIX_MATMUL_ROWS == 0 and MIX_MATMUL_ROWS % CONV_ROW_CHUNK == 0
    assert CONV_HALO >= CONV_WIDTH - 1 and CONV_HALO % SUBLANES == 0
    row = lambda v: v.reshape(1, -1)

    h = x
    for l in range(depth):
        k, qvt, kmean, hglu, w_out_b = _in_proj(h, row(g_mix_pre[l]), w_in[l].astype(BF16), row(b_glu[l]),
                                                w_out[l])
        kmean = kmean.reshape(B, S // MOBA_BLOCK, ATTN_WIDTH)
        attn = _moba(k, qvt, kmean)
        h1, f, w1b, w2b = _mix(h, attn, hglu, w_dw[l], row(b_dw[l]), row(ln_conv_g[l]), row(ln_conv_b[l]),
                               w_out_b, row(g_mix_post[l]), row(g_ffn_pre[l]), w_ff1[l], w_ff2[l])
        h = _ffn(f, h1, w1b, w2b, row(g_ffn_post[l]))
    return h
```

```python
import functools

import numpy as np
import jax
import jax.numpy as jnp
from jax import lax
from jax.experimental import pallas as pl
from jax.experimental.pallas import tpu as pltpu

F32 = jnp.float32
BF16 = jnp.bfloat16

HEAD_DIM = 128
N_HEADS = 8
ATTN_WIDTH = N_HEADS * HEAD_DIM
MOBA_BLOCK = 256
MOBA_TOPK = 3
CONV_WIDTH = 31
RMS_EPS = 1e-6
LN_EPS = 1e-5

V7X_VMEM_LIMIT_BYTES = 60000 * 1024
SUBLANES = 8
LANES = 128

IN_PROJ_ROWS = 512
IN_PROJ_COLS = 512
MIX_ROWS = 512
MIX_MATMUL_ROWS = 256
MIX_COLS = 512
CONV_HALO = 32
CONV_ROW_CHUNK = 128
CONV_LANE_CHUNK = 128
LN_ROW_CHUNK = 64
ATTN_UNROLL = 2
ATTN_KEY_GROUP = 2
ATTN_INIT_TILES = 4
ATTN_BODY_STEPS = 16
FFN_ROWS = 512
FFN_COLS = 1024

LOG2E = float(np.log2(np.e))
Q_SCALE = HEAD_DIM ** -0.5 * LOG2E
ALIBI_PARTS = 3
V_PAD_ROWS = 16
MASKED = -1e30
NT_DIMS = (((1,), (1,)), ((), ()))


def _resident(shape):
    return pl.BlockSpec(shape, lambda *_: (0,) * len(shape), pipeline_mode=pl.Buffered(1))


def _alibi_slopes(n_heads):
    return (2.0 ** (-8.0 * np.arange(1, n_heads + 1) / n_heads)).astype(np.float32)


def _in_proj_kernel(x_ref, g_ref, w_ref, bglu_ref, wo_ref,
                    k_ref, qvt_ref, kmean_ref, h_ref, wob_ref, wqvt_ref, *, scale):
    rows = x_ref.shape[1]
    conv_ch = h_ref.shape[2]
    k_col, v_col, glu_col = ATTN_WIDTH, 2 * ATTN_WIDTH, 3 * ATTN_WIDTH

    wob_ref[...] = wo_ref[...].astype(BF16)

    @pl.when(jnp.logical_and(pl.program_id(0) == 0, pl.program_id(1) == 0))
    def _():
        for rc in range(0, 2 * ATTN_WIDTH, MOBA_BLOCK):
            col = rc if rc < ATTN_WIDTH else v_col + rc - ATTN_WIDTH
            wqvt_ref[rc:rc + MOBA_BLOCK, :] = w_ref[:, col:col + MOBA_BLOCK].astype(F32).T.astype(BF16)

    x = x_ref[0]
    ms = jnp.mean(x * x, axis=-1, keepdims=True)
    a = (x * lax.rsqrt(ms + RMS_EPS) * g_ref[...]).astype(BF16)

    for c in range(0, ATTN_WIDTH, IN_PROJ_COLS):
        r = jnp.dot(a, w_ref[:, k_col + c:k_col + c + IN_PROJ_COLS], preferred_element_type=F32)
        blocks = r.reshape(rows // MOBA_BLOCK, MOBA_BLOCK, IN_PROJ_COLS)
        kmean_ref[0, 0, :, c:c + IN_PROJ_COLS] = jnp.mean(blocks, axis=1)
        k_ref[0, :, c:c + IN_PROJ_COLS] = r.astype(BF16)

    for rc in range(0, 2 * ATTN_WIDTH, MOBA_BLOCK):
        r = lax.dot_general(wqvt_ref[rc:rc + MOBA_BLOCK, :], a, NT_DIMS,
                            preferred_element_type=F32)
        if rc < ATTN_WIDTH:
            r = r * scale
        r = r.astype(BF16)
        for blk in range(rows // MOBA_BLOCK):
            qvt_ref[0, blk, rc:rc + MOBA_BLOCK, :] = r[:, blk * MOBA_BLOCK:(blk + 1) * MOBA_BLOCK]

    for c in range(0, conv_ch, IN_PROJ_COLS):
        val = jnp.dot(a, w_ref[:, glu_col + c:glu_col + c + IN_PROJ_COLS], preferred_element_type=F32)
        val = val + bglu_ref[:, c:c + IN_PROJ_COLS]
        gate_col = glu_col + conv_ch + c
        gt = jnp.dot(a, w_ref[:, gate_col:gate_col + IN_PROJ_COLS], preferred_element_type=F32)
        gt = gt + bglu_ref[:, conv_ch + c:conv_ch + c + IN_PROJ_COLS]
        h_ref[0, :, c:c + IN_PROJ_COLS] = val * (1.0 / (1.0 + jnp.exp(-gt)))


def _in_proj(x, g, w, b_glu, w_out):
    B, S, D = x.shape
    conv_ch = b_glu.shape[1] // 2
    assert w.shape == (D, 3 * ATTN_WIDTH + 2 * conv_ch)
    rows = IN_PROJ_ROWS
    n_blk = rows // MOBA_BLOCK
    tiles_per_seq = S // rows
    grid = (B, tiles_per_seq)
    wo_slice = w_out.shape[0] // (B * tiles_per_seq)
    assert wo_slice * B * tiles_per_seq == w_out.shape[0] and wo_slice % (2 * SUBLANES) == 0
    out_shape = (
        jax.ShapeDtypeStruct((B, S, ATTN_WIDTH), BF16),
        jax.ShapeDtypeStruct((B, S // MOBA_BLOCK, 2 * ATTN_WIDTH, MOBA_BLOCK), BF16),
        jax.ShapeDtypeStruct((B, S // rows, n_blk, ATTN_WIDTH), F32),
        jax.ShapeDtypeStruct((B, S, conv_ch), F32),
        jax.ShapeDtypeStruct(w_out.shape, BF16),
    )
    return pl.pallas_call(
        functools.partial(_in_proj_kernel, scale=Q_SCALE),
        grid=grid,
        in_specs=[
            pl.BlockSpec((1, rows, D), lambda b, i: (b, i, 0)),
            _resident((1, D)),
            _resident(w.shape),
            _resident((1, 2 * conv_ch)),
            pl.BlockSpec((wo_slice, w_out.shape[1]), lambda b, i: (b * tiles_per_seq + i, 0)),
        ],
        out_specs=(
            pl.BlockSpec((1, rows, ATTN_WIDTH), lambda b, i: (b, i, 0)),
            pl.BlockSpec((1, n_blk, 2 * ATTN_WIDTH, MOBA_BLOCK), lambda b, i: (b, i, 0, 0)),
            pl.BlockSpec((1, 1, n_blk, ATTN_WIDTH), lambda b, i: (b, i, 0, 0)),
            pl.BlockSpec((1, rows, conv_ch), lambda b, i: (b, i, 0)),
            pl.BlockSpec((wo_slice, w_out.shape[1]), lambda b, i: (b * tiles_per_seq + i, 0)),
        ),
        out_shape=out_shape,
        scratch_shapes=[pltpu.VMEM((2 * ATTN_WIDTH, D), BF16)],
        compiler_params=pltpu.CompilerParams(
            dimension_semantics=("arbitrary", "arbitrary"),
            vmem_limit_bytes=V7X_VMEM_LIMIT_BYTES),
        name="in_proj",
    )(x, g, w, b_glu, w_out)


def _pair_schedule(n_blocks, unroll, group):
    todo = {}
    for qi in range(1, n_blocks):
        blocks = list(range(qi)) + [-1] * (-qi % group)
        todo[qi] = [blocks[i:i + group] for i in range(0, len(blocks), group)]
    key_blocks, query_tiles = [], []
    while any(todo.values()):
        tiles = sorted((qi for qi in todo if todo[qi]), key=lambda t: -len(todo[t]))[:unroll]
        for qi in tiles:
            key_blocks += todo[qi].pop(0)
            query_tiles.append(qi)
        for _ in range(unroll - len(tiles)):
            key_blocks += [-1] * group
            query_tiles.append(n_blocks)
    while (len(query_tiles) // unroll) % ATTN_BODY_STEPS:
        key_blocks += [-1] * group * unroll
        query_tiles += [n_blocks] * unroll
    return np.asarray(key_blocks, np.int32), np.asarray(query_tiles, np.int32)


def _moba_kernel(slopes_ref, pair_k_ref, pair_q_ref, qext_ref, qt_ref, k_ref, vt_ref, kmean_ref, o_ref,
                 qaug_ref, kaug_ref, vaug_ref, pen_ref, m_ref, acc_ref,
                 s_even_ref, s_odd_ref, smax_even_ref, smax_odd_ref):
    L = MOBA_BLOCK
    U = ATTN_UNROLL
    G = ATTN_KEY_GROUP
    T = ATTN_INIT_TILES
    Dh = HEAD_DIM
    n_blocks = vt_ref.shape[1]
    n_steps = pair_q_ref.shape[0] // U
    slope = slopes_ref[pl.program_id(1)]

    lane = lax.broadcasted_iota(jnp.int32, (L, Dh), 1)
    key_cols = jnp.where(lane < ALIBI_PARTS, lax.broadcasted_iota(jnp.int32, (L, Dh), 0), 0).astype(BF16)
    ones_row = (lax.broadcasted_iota(jnp.int32, (V_PAD_ROWS, L), 0) == 0).astype(BF16)
    for blk in range(n_blocks):
        kaug_ref[blk * L:(blk + 1) * L, 0:Dh] = k_ref[0, blk * L:(blk + 1) * L, :]
        kaug_ref[blk * L:(blk + 1) * L, Dh:2 * Dh] = key_cols
        qaug_ref[blk, 0:Dh, :] = qt_ref[0, blk]
        qaug_ref[blk, Dh:2 * Dh, :] = qext_ref[0]
        vaug_ref[blk, 0:Dh, :] = vt_ref[0, blk]
        vaug_ref[blk, Dh:Dh + V_PAD_ROWS, :] = ones_row

    key_local = lax.broadcasted_iota(jnp.int32, (L, L), 0)
    qry_local = lax.broadcasted_iota(jnp.int32, (L, L), 1)
    kmean = kmean_ref[0].astype(BF16)
    blk_id = lax.broadcasted_iota(jnp.int32, (n_blocks, L), 0).astype(F32)

    def init_tiles(g, carry):
        tiles = [g * T + u for u in range(T)]
        for u, qi in enumerate(tiles):
            k_own = kaug_ref[pl.ds(pl.multiple_of(qi * L, L), L), :]
            s_even_ref[u] = jnp.dot(k_own, qaug_ref[qi], preferred_element_type=F32)
        gates = [jnp.dot(kmean, qt_ref[0, qi], preferred_element_type=F32) for qi in tiles]
        for u, qi in enumerate(tiles):
            gate = jnp.where(blk_id < jnp.asarray(qi, F32), gates[u], -jnp.inf)
            sel = jnp.zeros((n_blocks, L), F32)
            for r in range(MOBA_TOPK):
                best = jnp.max(gate, axis=0, keepdims=True)
                first = jnp.min(jnp.where(gate == best, blk_id, float(n_blocks)), axis=0, keepdims=True)
                hit = blk_id == first
                sel = jnp.maximum(sel, jnp.where(hit, jnp.asarray(qi > r, F32), 0.0))
                gate = jnp.where(hit, -jnp.inf, gate)
            pen_ref[qi] = jnp.where(sel > 0.0, 0.0, MASKED)

            s = jnp.where(key_local <= qry_local, s_even_ref[u], -jnp.inf)
            m = jnp.max(s, axis=0, keepdims=True)
            p = jnp.exp2(s - m)
            m_ref[qi] = m
            acc_ref[qi] = jnp.dot(vaug_ref[qi], p.astype(BF16), preferred_element_type=F32)
        return carry

    lax.fori_loop(0, n_blocks // T, init_tiles, 0)
    pen_ref[n_blocks] = jnp.full((n_blocks, L), MASKED, F32)
    m_ref[n_blocks] = jnp.zeros((1, L), F32)
    acc_ref[n_blocks] = jnp.zeros(acc_ref.shape[1:], F32)

    def key_block(step, u, t):
        j = pair_k_ref[(step * U + u) * G + t]
        return jnp.maximum(j, 0), j >= 0

    def scores(step, s_ref, smax_ref):
        for u in range(U):
            qi = jnp.minimum(pair_q_ref[step * U + u], n_blocks - 1)
            for t in range(G):
                j, _ = key_block(step, u, t)
                k_blk = kaug_ref[pl.ds(pl.multiple_of(j * L, L), L), :]
                s = jnp.dot(k_blk, qaug_ref[qi], preferred_element_type=F32)
                s_ref[u * G + t] = s
                smax_ref[u * G + t] = jnp.max(s, axis=0, keepdims=True)

    def update(step, s_ref, smax_ref):
        loaded = []
        for u in range(U):
            slot = pair_q_ref[step * U + u]
            blocks = []
            for t in range(G):
                j, present = key_block(step, u, t)
                row = pen_ref[slot, pl.ds(j, 1), :] - slope * float(L) * jnp.asarray(slot - j, F32)
                blocks.append((j, jnp.where(present, row, MASKED)))
            loaded.append((u, slot, blocks, m_ref[slot]))
        updated = []
        for u, slot, blocks, m in loaded:
            m_new = m
            for t, (j, row) in enumerate(blocks):
                m_new = jnp.maximum(m_new, smax_ref[u * G + t] + row)
            alpha = jnp.exp2(m - m_new)
            pv = None
            for t, (j, row) in enumerate(blocks):
                p = jnp.exp2(s_ref[u * G + t] + (row - m_new))
                part = jnp.dot(vaug_ref[j], p.astype(BF16), preferred_element_type=F32)
                pv = part if pv is None else pv + part
            acc_ref[slot] = alpha * acc_ref[slot] + pv
            updated.append((slot, m_new))
        for slot, m in updated:
            m_ref[slot] = m

    def several_steps(t, carry):
        for i in range(ATTN_BODY_STEPS):
            step = ATTN_BODY_STEPS * t + i
            cur, nxt = (even, odd) if i % 2 == 0 else (odd, even)
            scores(jnp.minimum(step + 1, n_steps - 1), *nxt)
            update(step, *cur)
        return carry

    even, odd = (s_even_ref, smax_even_ref), (s_odd_ref, smax_odd_ref)
    scores(0, *even)
    lax.fori_loop(0, n_steps // ATTN_BODY_STEPS, several_steps, 0)

    def finish_tiles(g, carry):
        for u in range(T):
            qi = g * T + u
            q0 = pl.multiple_of(qi * L, L)
            acc = acc_ref[qi]
            o_ref[0, pl.ds(q0, L), :] = (acc[0:Dh] / acc[Dh:Dh + 1]).T.astype(o_ref.dtype)
        return carry

    lax.fori_loop(0, n_blocks // T, finish_tiles, 0)


def _bf16_parts(x, n):
    parts, rest = [], np.asarray(x, np.float32)
    for _ in range(n):
        part = rest.astype(BF16).astype(np.float32)
        parts.append(part)
        rest = (rest - part).astype(np.float32)
    return np.stack(parts, axis=-1)


def _moba(k, qvt, kmean):
    B, S, _ = k.shape
    n_blocks = S // MOBA_BLOCK
    assert n_blocks % ATTN_INIT_TILES == 0 and ATTN_BODY_STEPS % 2 == 0
    assert ATTN_UNROLL * ATTN_KEY_GROUP >= ATTN_INIT_TILES
    pair_k, pair_q = _pair_schedule(n_blocks, ATTN_UNROLL, ATTN_KEY_GROUP)
    n_tiles = ATTN_UNROLL * ATTN_KEY_GROUP
    slopes = (_alibi_slopes(N_HEADS) * LOG2E).astype(np.float32)
    qext = np.zeros((N_HEADS, HEAD_DIM, MOBA_BLOCK), np.float32)
    qext[:, :ALIBI_PARTS, :] = _bf16_parts(slopes, ALIBI_PARTS)[:, :, None]
    smem = pl.BlockSpec(memory_space=pltpu.SMEM)
    return pl.pallas_call(
        _moba_kernel,
        grid=(B, N_HEADS),
        in_specs=[
            smem, smem, smem,
            pl.BlockSpec((1, HEAD_DIM, MOBA_BLOCK), lambda b, h: (h, 0, 0)),
            pl.BlockSpec((1, n_blocks, HEAD_DIM, MOBA_BLOCK), lambda b, h: (b, 0, h, 0)),
            pl.BlockSpec((1, S, HEAD_DIM), lambda b, h: (b, 0, h)),
            pl.BlockSpec((1, n_blocks, HEAD_DIM, MOBA_BLOCK), lambda b, h: (b, 0, N_HEADS + h, 0)),
            pl.BlockSpec((1, n_blocks, HEAD_DIM), lambda b, h: (b, 0, h)),
        ],
        out_specs=pl.BlockSpec((1, S, HEAD_DIM), lambda b, h: (b, 0, h)),
        out_shape=jax.ShapeDtypeStruct((B, S, ATTN_WIDTH), BF16),
        scratch_shapes=[
            pltpu.VMEM((n_blocks, 2 * HEAD_DIM, MOBA_BLOCK), BF16),
            pltpu.VMEM((S, 2 * HEAD_DIM), BF16),
            pltpu.VMEM((n_blocks, HEAD_DIM + V_PAD_ROWS, MOBA_BLOCK), BF16),
            pltpu.VMEM((n_blocks + 1, n_blocks, MOBA_BLOCK), F32),
            pltpu.VMEM((n_blocks + 1, 1, MOBA_BLOCK), F32),
            pltpu.VMEM((n_blocks + 1, HEAD_DIM + V_PAD_ROWS, MOBA_BLOCK), F32),
            pltpu.VMEM((n_tiles, MOBA_BLOCK, MOBA_BLOCK), F32),
            pltpu.VMEM((n_tiles, MOBA_BLOCK, MOBA_BLOCK), F32),
            pltpu.VMEM((n_tiles, 1, MOBA_BLOCK), F32),
            pltpu.VMEM((n_tiles, 1, MOBA_BLOCK), F32),
        ],
        compiler_params=pltpu.CompilerParams(
            dimension_semantics=("arbitrary", "arbitrary"),
            vmem_limit_bytes=V7X_VMEM_LIMIT_BYTES),
        name="moba",
    )(jnp.asarray(slopes), jnp.asarray(pair_k), jnp.asarray(pair_q), jnp.asarray(qext, BF16),
      qvt, k, qvt, kmean)


def _mix_kernel(x_ref, attn_ref, hmain_ref, hhalo_ref, wdw_ref, bdw_ref, lng_ref, lnb_ref,
                wo_ref, gpost_ref, gpre_ref, w1_ref, w2_ref,
                h1_ref, f_ref, w1b_ref, w2b_ref, ext_ref, y_ref, conv_ref, mixed_ref):
    rows = x_ref.shape[1]
    conv_ch = hmain_ref.shape[2]
    first_tile = pl.program_id(1) == 0

    w1b_ref[...] = w1_ref[...].astype(BF16)
    w2b_ref[...] = w2_ref[...].astype(BF16)

    ext_ref[0:CONV_HALO, :] = jnp.where(first_tile, 0.0, hhalo_ref[0])
    ext_ref[CONV_HALO:CONV_HALO + rows, :] = hmain_ref[0]

    attn = attn_ref[0]
    attn_w = attn.shape[1]
    d_model = wo_ref.shape[1]
    for c in range(0, d_model, MIX_COLS):
        mixed_ref[:, c:c + MIX_COLS] = jnp.dot(attn, wo_ref[0:attn_w, c:c + MIX_COLS],
                                               preferred_element_type=F32)

    shift0 = CONV_HALO - (CONV_WIDTH - 1)
    R, CL = CONV_ROW_CHUNK, CONV_LANE_CHUNK
    for r0 in range(0, rows, MIX_MATMUL_ROWS):
        for rc in range(r0, r0 + MIX_MATMUL_ROWS, R):
            for c0 in range(0, conv_ch, CL):
                y = None
                for b in range(SUBLANES):
                    z_rows = R if b == 0 else R + SUBLANES
                    z = None
                    for a in range((shift0 + CONV_WIDTH - 1) // SUBLANES + 1):
                        w = SUBLANES * a + b - shift0
                        if 0 <= w < CONV_WIDTH:
                            lo = rc + SUBLANES * a
                            tap = ext_ref[lo:lo + z_rows, c0:c0 + CL] * wdw_ref[w:w + 1, c0:c0 + CL]
                            z = tap if z is None else z + tap
                    part = z[b:b + R]
                    y = part if y is None else y + part
                y_ref[rc:rc + R, c0:c0 + CL] = y + bdw_ref[:, c0:c0 + CL]

        for rl in range(r0, r0 + MIX_MATMUL_ROWS, LN_ROW_CHUNK):
            y = y_ref[rl:rl + LN_ROW_CHUNK, :]
            mu = jnp.mean(y, axis=-1, keepdims=True)
            var = jnp.mean(jnp.square(y - mu), axis=-1, keepdims=True)
            y = (y - mu) * lax.rsqrt(var + LN_EPS) * lng_ref[...] + lnb_ref[...]
            y = y * (1.0 / (1.0 + jnp.exp(-y)))
            conv_ref[rl:rl + LN_ROW_CHUNK, :] = y.astype(BF16)

        conv = conv_ref[r0:r0 + MIX_MATMUL_ROWS, :]
        cols = []
        for c in range(0, d_model, MIX_COLS):
            r = jnp.dot(conv, wo_ref[attn_w:attn_w + conv_ch, c:c + MIX_COLS], preferred_element_type=F32)
            cols.append(mixed_ref[r0:r0 + MIX_MATMUL_ROWS, c:c + MIX_COLS] + r)
        mixed = jnp.concatenate(cols, axis=-1)
        ms = jnp.mean(mixed * mixed, axis=-1, keepdims=True)
        h1 = x_ref[0, r0:r0 + MIX_MATMUL_ROWS, :] + mixed * lax.rsqrt(ms + RMS_EPS) * gpost_ref[...]
        h1_ref[0, r0:r0 + MIX_MATMUL_ROWS, :] = h1
        ms1 = jnp.mean(h1 * h1, axis=-1, keepdims=True)
        f_ref[0, r0:r0 + MIX_MATMUL_ROWS, :] = (h1 * lax.rsqrt(ms1 + RMS_EPS) * gpre_ref[...]).astype(BF16)


def _mix(x, attn, hglu, w_dw, b_dw, ln_g, ln_b, w_out, g_post, g_pre, w_ff1, w_ff2):
    B, S, D = x.shape
    conv_ch = hglu.shape[2]
    rows = MIX_ROWS
    halo_per_tile = rows // CONV_HALO
    tiles_per_seq = S // rows
    d_ff = w_ff1.shape[1]
    ff_slice = d_ff // (B * tiles_per_seq)
    assert ff_slice * B * tiles_per_seq == d_ff and ff_slice % LANES == 0
    return pl.pallas_call(
        _mix_kernel,
        grid=(B, tiles_per_seq),
        in_specs=[
            pl.BlockSpec((1, rows, D), lambda b, i: (b, i, 0)),
            pl.BlockSpec((1, rows, attn.shape[2]), lambda b, i: (b, i, 0)),
            pl.BlockSpec((1, rows, conv_ch), lambda b, i: (b, i, 0)),
            pl.BlockSpec((1, CONV_HALO, conv_ch),
                         lambda b, i: (b, jnp.maximum(i * halo_per_tile - 1, 0), 0)),
            _resident(w_dw.shape),
            _resident((1, conv_ch)),
            _resident((1, conv_ch)),
            _resident((1, conv_ch)),
            _resident(w_out.shape),
            _resident((1, D)),
            _resident((1, D)),
            pl.BlockSpec((D, ff_slice), lambda b, i: (0, b * tiles_per_seq + i)),
            pl.BlockSpec((ff_slice, D), lambda b, i: (b * tiles_per_seq + i, 0)),
        ],
        out_specs=(
            pl.BlockSpec((1, rows, D), lambda b, i: (b, i, 0)),
            pl.BlockSpec((1, rows, D), lambda b, i: (b, i, 0)),
            pl.BlockSpec((D, ff_slice), lambda b, i: (0, b * tiles_per_seq + i)),
            pl.BlockSpec((ff_slice, D), lambda b, i: (b * tiles_per_seq + i, 0)),
        ),
        out_shape=(jax.ShapeDtypeStruct((B, S, D), F32), jax.ShapeDtypeStruct((B, S, D), BF16),
                   jax.ShapeDtypeStruct(w_ff1.shape, BF16), jax.ShapeDtypeStruct(w_ff2.shape, BF16)),
        scratch_shapes=[
            pltpu.VMEM((CONV_HALO + rows, conv_ch), F32),
            pltpu.VMEM((rows, conv_ch), F32),
            pltpu.VMEM((rows, conv_ch), BF16),
            pltpu.VMEM((rows, D), F32),
        ],
        compiler_params=pltpu.CompilerParams(
            dimension_semantics=("arbitrary", "arbitrary"),
            vmem_limit_bytes=V7X_VMEM_LIMIT_BYTES),
        name="mix",
    )(x, attn, hglu, hglu, w_dw, b_dw, ln_g, ln_b, w_out, g_post, g_pre, w_ff1, w_ff2)


def _ffn_kernel(f_ref, h1_ref, w1_ref, w2_ref, g_ref, o_ref, acc_ref):
    kf = pl.program_id(2)

    @pl.when(kf == 0)
    def _():
        acc_ref[...] = jnp.zeros_like(acc_ref)

    u = jnp.dot(f_ref[0], w1_ref[...], preferred_element_type=F32)
    u = jnp.square(jnp.maximum(u, 0.0)).astype(BF16)
    acc_ref[...] += jnp.dot(u, w2_ref[...], preferred_element_type=F32)

    @pl.when(kf == pl.num_programs(2) - 1)
    def _():
        a = acc_ref[...]
        ms = jnp.mean(a * a, axis=-1, keepdims=True)
        o_ref[0] = h1_ref[0] + a * lax.rsqrt(ms + RMS_EPS) * g_ref[...]


def _ffn(f, h1, w1, w2, g):
    B, S, D = h1.shape
    d_ff = w1.shape[1]
    rows, cols = FFN_ROWS, FFN_COLS
    return pl.pallas_call(
        _ffn_kernel,
        grid=(B, S // rows, d_ff // cols),
        in_specs=[
            pl.BlockSpec((1, rows, D), lambda b, i, k: (b, i, 0)),
            pl.BlockSpec((1, rows, D), lambda b, i, k: (b, i, 0)),
            pl.BlockSpec((D, cols), lambda b, i, k: (0, k)),
            pl.BlockSpec((cols, D), lambda b, i, k: (k, 0)),
            _resident((1, D)),
        ],
        out_specs=pl.BlockSpec((1, rows, D), lambda b, i, k: (b, i, 0)),
        out_shape=jax.ShapeDtypeStruct((B, S, D), F32),
        scratch_shapes=[pltpu.VMEM((rows, D), F32)],
        compiler_params=pltpu.CompilerParams(
            dimension_semantics=("arbitrary", "arbitrary", "arbitrary"),
            vmem_limit_bytes=V7X_VMEM_LIMIT_BYTES),
        name="ffn",
    )(f, h1, w1, w2, g)


def kernel(x, g_mix_pre, w_in, b_glu, w_dw, b_dw, ln_conv_g, ln_conv_b, w_out, g_mix_post,
           g_ffn_pre, w_ff1, w_ff2, g_ffn_post):
    B, S, D = x.shape
    depth = w_in.shape[0]
    assert S % IN_PROJ_ROWS == 0 and S % MIX_ROWS == 0 and S % FFN_ROWS == 0
    assert IN_PROJ_ROWS % MOBA_BLOCK == 0 and MIX_ROWS % CONV_HALO == 0
    assert MIX_ROWS % MIX_MATMUL_ROWS == 0 and MIX_MATMUL_ROWS % CONV_ROW_CHUNK == 0
    assert CONV_HALO >= CONV_WIDTH - 1 and CONV_HALO % SUBLANES == 0
    row = lambda v: v.reshape(1, -1)

    h = x
    for l in range(depth):
        k, qvt, kmean, hglu, w_out_b = _in_proj(h, row(g_mix_pre[l]), w_in[l].astype(BF16), row(b_glu[l]),
                                                w_out[l])
        kmean = kmean.reshape(B, S // MOBA_BLOCK, ATTN_WIDTH)
        attn = _moba(k, qvt, kmean)
        h1, f, w1b, w2b = _mix(h, attn, hglu, w_dw[l], row(b_dw[l]), row(ln_conv_g[l]), row(ln_conv_b[l]),
                               w_out_b, row(g_mix_post[l]), row(g_ffn_pre[l]), w_ff1[l], w_ff2[l])
        h = _ffn(f, h1, w1b, w2b, row(g_ffn_post[l]))
    return h
```

```python
import functools

import numpy as np
import jax
import jax.numpy as jnp
from jax import lax
from jax.experimental import pallas as pl
from jax.experimental.pallas import tpu as pltpu

F32 = jnp.float32
BF16 = jnp.bfloat16

HEAD_DIM = 128
N_HEADS = 8
ATTN_WIDTH = N_HEADS * HEAD_DIM
MOBA_BLOCK = 256
MOBA_TOPK = 3
CONV_WIDTH = 31
RMS_EPS = 1e-6
LN_EPS = 1e-5

V7X_VMEM_LIMIT_BYTES = 60000 * 1024
SUBLANES = 8
LANES = 128

IN_PROJ_ROWS = 512
IN_PROJ_COLS = 512
MIX_ROWS = 512
MIX_MATMUL_ROWS = 256
MIX_COLS = 512
CONV_HALO = 32
CONV_ROW_CHUNK = 128
CONV_LANE_CHUNK = 128
LN_ROW_CHUNK = 64
ATTN_UNROLL = 2
ATTN_KEY_GROUP = 2
ATTN_INIT_TILES = 4
ATTN_BODY_STEPS = 16
FFN_ROWS = 512
FFN_COLS = 1024

LOG2E = float(np.log2(np.e))
Q_SCALE = HEAD_DIM ** -0.5 * LOG2E
ALIBI_PARTS = 3
V_PAD_ROWS = 16
MASKED = -1e30
NT_DIMS = (((1,), (1,)), ((), ()))


def _resident(shape):
    return pl.BlockSpec(shape, lambda *_: (0,) * len(shape), pipeline_mode=pl.Buffered(1))


def _alibi_slopes(n_heads):
    return (2.0 ** (-8.0 * np.arange(1, n_heads + 1) / n_heads)).astype(np.float32)


def _in_proj_kernel(x_ref, g_ref, w_ref, bglu_ref, wo_ref,
                    k_ref, qvt_ref, kmean_ref, h_ref, wob_ref, wqvt_ref, *, scale):
    rows = x_ref.shape[1]
    conv_ch = h_ref.shape[2]
    k_col, v_col, glu_col = ATTN_WIDTH, 2 * ATTN_WIDTH, 3 * ATTN_WIDTH

    wob_ref[...] = wo_ref[...].astype(BF16)

    @pl.when(jnp.logical_and(pl.program_id(0) == 0, pl.program_id(1) == 0))
    def _():
        for rc in range(0, 2 * ATTN_WIDTH, MOBA_BLOCK):
            col = rc if rc < ATTN_WIDTH else v_col + rc - ATTN_WIDTH
            wqvt_ref[rc:rc + MOBA_BLOCK, :] = w_ref[:, col:col + MOBA_BLOCK].astype(F32).T.astype(BF16)

    x = x_ref[0]
    ms = jnp.mean(x * x, axis=-1, keepdims=True)
    a = (x * lax.rsqrt(ms + RMS_EPS) * g_ref[...]).astype(BF16)

    for c in range(0, ATTN_WIDTH, IN_PROJ_COLS):
        r = jnp.dot(a, w_ref[:, k_col + c:k_col + c + IN_PROJ_COLS], preferred_element_type=F32)
        blocks = r.reshape(rows // MOBA_BLOCK, MOBA_BLOCK, IN_PROJ_COLS)
        kmean_ref[0, 0, :, c:c + IN_PROJ_COLS] = jnp.mean(blocks, axis=1)
        k_ref[0, :, c:c + IN_PROJ_COLS] = r.astype(BF16)

    for rc in range(0, 2 * ATTN_WIDTH, MOBA_BLOCK):
        r = lax.dot_general(wqvt_ref[rc:rc + MOBA_BLOCK, :], a, NT_DIMS,
                            preferred_element_type=F32)
        if rc < ATTN_WIDTH:
            r = r * scale
        r = r.astype(BF16)
        for blk in range(rows // MOBA_BLOCK):
            qvt_ref[0, blk, rc:rc + MOBA_BLOCK, :] = r[:, blk * MOBA_BLOCK:(blk + 1) * MOBA_BLOCK]

    for c in range(0, conv_ch, IN_PROJ_COLS):
        val = jnp.dot(a, w_ref[:, glu_col + c:glu_col + c + IN_PROJ_COLS], preferred_element_type=F32)
        val = val + bglu_ref[:, c:c + IN_PROJ_COLS]
        gate_col = glu_col + conv_ch + c
        gt = jnp.dot(a, w_ref[:, gate_col:gate_col + IN_PROJ_COLS], preferred_element_type=F32)
        gt = gt + bglu_ref[:, conv_ch + c:conv_ch + c + IN_PROJ_COLS]
        h_ref[0, :, c:c + IN_PROJ_COLS] = val * (1.0 / (1.0 + jnp.exp(-gt)))


def _in_proj(x, g, w, b_glu, w_out):
    B, S, D = x.shape
    conv_ch = b_glu.shape[1] // 2
    assert w.shape == (D, 3 * ATTN_WIDTH + 2 * conv_ch)
    rows = IN_PROJ_ROWS
    n_blk = rows // MOBA_BLOCK
    tiles_per_seq = S // rows
    grid = (B, tiles_per_seq)
    wo_slice = w_out.shape[0] // (B * tiles_per_seq)
    assert wo_slice * B * tiles_per_seq == w_out.shape[0] and wo_slice % (2 * SUBLANES) == 0
    out_shape = (
        jax.ShapeDtypeStruct((B, S, ATTN_WIDTH), BF16),
        jax.ShapeDtypeStruct((B, S // MOBA_BLOCK, 2 * ATTN_WIDTH, MOBA_BLOCK), BF16),
        jax.ShapeDtypeStruct((B, S // rows, n_blk, ATTN_WIDTH), F32),
        jax.ShapeDtypeStruct((B, S, conv_ch), F32),
        jax.ShapeDtypeStruct(w_out.shape, BF16),
    )
    return pl.pallas_call(
        functools.partial(_in_proj_kernel, scale=Q_SCALE),
        grid=grid,
        in_specs=[
            pl.BlockSpec((1, rows, D), lambda b, i: (b, i, 0)),
            _resident((1, D)),
            _resident(w.shape),
            _resident((1, 2 * conv_ch)),
            pl.BlockSpec((wo_slice, w_out.shape[1]), lambda b, i: (b * tiles_per_seq + i, 0)),
        ],
        out_specs=(
            pl.BlockSpec((1, rows, ATTN_WIDTH), lambda b, i: (b, i, 0)),
            pl.BlockSpec((1, n_blk, 2 * ATTN_WIDTH, MOBA_BLOCK), lambda b, i: (b, i, 0, 0)),
            pl.BlockSpec((1, 1, n_blk, ATTN_WIDTH), lambda b, i: (b, i, 0, 0)),
            pl.BlockSpec((1, rows, conv_ch), lambda b, i: (b, i, 0)),
            pl.BlockSpec((wo_slice, w_out.shape[1]), lambda b, i: (b * tiles_per_seq + i, 0)),
        ),
        out_shape=out_shape,
        scratch_shapes=[pltpu.VMEM((2 * ATTN_WIDTH, D), BF16)],
        compiler_params=pltpu.CompilerParams(
            dimension_semantics=("arbitrary", "arbitrary"),
            vmem_limit_bytes=V7X_VMEM_LIMIT_BYTES),
        name="in_proj",
    )(x, g, w, b_glu, w_out)


def _pair_schedule(n_blocks, unroll, group):
    todo = {}
    for qi in range(1, n_blocks):
        blocks = list(range(qi)) + [-1] * (-qi % group)
        todo[qi] = [blocks[i:i + group] for i in range(0, len(blocks), group)]
    key_blocks, query_tiles = [], []
    while any(todo.values()):
        tiles = sorted((qi for qi in todo if todo[qi]), key=lambda t: -len(todo[t]))[:unroll]
        for qi in tiles:
            key_blocks += todo[qi].pop(0)
            query_tiles.append(qi)
        for _ in range(unroll - len(tiles)):
            key_blocks += [-1] * group
            query_tiles.append(n_blocks)
    while (len(query_tiles) // unroll) % ATTN_BODY_STEPS:
        key_blocks += [-1] * group * unroll
        query_tiles += [n_blocks] * unroll
    return np.asarray(key_blocks, np.int32), np.asarray(query_tiles, np.int32)


def _moba_kernel(slopes_ref, pair_k_ref, pair_q_ref, qext_ref, qt_ref, k_ref, vt_ref, kmean_ref, o_ref,
                 qaug_ref, kaug_ref, vaug_ref, pen_ref, m_ref, acc_ref,
                 s_even_ref, s_odd_ref, smax_even_ref, smax_odd_ref):
    L = MOBA_BLOCK
    U = ATTN_UNROLL
    G = ATTN_KEY_GROUP
    T = ATTN_INIT_TILES
    Dh = HEAD_DIM
    n_blocks = vt_ref.shape[1]
    n_steps = pair_q_ref.shape[0] // U
    slope = slopes_ref[pl.program_id(1)]

    lane = lax.broadcasted_iota(jnp.int32, (L, Dh), 1)
    key_cols = jnp.where(lane < ALIBI_PARTS, lax.broadcasted_iota(jnp.int32, (L, Dh), 0), 0).astype(BF16)
    ones_row = (lax.broadcasted_iota(jnp.int32, (V_PAD_ROWS, L), 0) == 0).astype(BF16)
    for blk in range(n_blocks):
        kaug_ref[blk * L:(blk + 1) * L, 0:Dh] = k_ref[0, blk * L:(blk + 1) * L, :]
        kaug_ref[blk * L:(blk + 1) * L, Dh:2 * Dh] = key_cols
        qaug_ref[blk, 0:Dh, :] = qt_ref[0, blk]
        qaug_ref[blk, Dh:2 * Dh, :] = qext_ref[0]
        vaug_ref[blk, 0:Dh, :] = vt_ref[0, blk]
        vaug_ref[blk, Dh:Dh + V_PAD_ROWS, :] = ones_row

    key_local = lax.broadcasted_iota(jnp.int32, (L, L), 0)
    qry_local = lax.broadcasted_iota(jnp.int32, (L, L), 1)
    kmean = kmean_ref[0].astype(BF16)
    blk_id = lax.broadcasted_iota(jnp.int32, (n_blocks, L), 0).astype(F32)

    def init_tiles(g, carry):
        tiles = [g * T + u for u in range(T)]
        for u, qi in enumerate(tiles):
            k_own = kaug_ref[pl.ds(pl.multiple_of(qi * L, L), L), :]
            s_even_ref[u] = jnp.dot(k_own, qaug_ref[qi], preferred_element_type=F32)
        gates = [jnp.dot(kmean, qt_ref[0, qi], preferred_element_type=F32) for qi in tiles]
        for u, qi in enumerate(tiles):
            gate = jnp.where(blk_id < jnp.asarray(qi, F32), gates[u], -jnp.inf)
            sel = jnp.zeros((n_blocks, L), F32)
            for r in range(MOBA_TOPK):
                best = jnp.max(gate, axis=0, keepdims=True)
                first = jnp.min(jnp.where(gate == best, blk_id, float(n_blocks)), axis=0, keepdims=True)
                hit = blk_id == first
                sel = jnp.maximum(sel, jnp.where(hit, jnp.asarray(qi > r, F32), 0.0))
                gate = jnp.where(hit, -jnp.inf, gate)
            pen_ref[qi] = jnp.where(sel > 0.0, 0.0, MASKED)

            s = jnp.where(key_local <= qry_local, s_even_ref[u], -jnp.inf)
            m = jnp.max(s, axis=0, keepdims=True)
            p = jnp.exp2(s - m)
            m_ref[qi] = m
            acc_ref[qi] = jnp.dot(vaug_ref[qi], p.astype(BF16), preferred_element_type=F32)
        return carry

    lax.fori_loop(0, n_blocks // T, init_tiles, 0)
    pen_ref[n_blocks] = jnp.full((n_blocks, L), MASKED, F32)
    m_ref[n_blocks] = jnp.zeros((1, L), F32)
    acc_ref[n_blocks] = jnp.zeros(acc_ref.shape[1:], F32)

    def key_block(step, u, t):
        j = pair_k_ref[(step * U + u) * G + t]
        return jnp.maximum(j, 0), j >= 0

    def scores(step, s_ref, smax_ref):
        for u in range(U):
            qi = jnp.minimum(pair_q_ref[step * U + u], n_blocks - 1)
            for t in range(G):
                j, _ = key_block(step, u, t)
                k_blk = kaug_ref[pl.ds(pl.multiple_of(j * L, L), L), :]
                s = jnp.dot(k_blk, qaug_ref[qi], preferred_element_type=F32)
                s_ref[u * G + t] = s
                smax_ref[u * G + t] = jnp.max(s, axis=0, keepdims=True)

    def update(step, s_ref, smax_ref):
        loaded = []
        for u in range(U):
            slot = pair_q_ref[step * U + u]
            blocks = []
            for t in range(G):
                j, present = key_block(step, u, t)
                row = pen_ref[slot, pl.ds(j, 1), :] - slope * float(L) * jnp.asarray(slot - j, F32)
                blocks.append((j, jnp.where(present, row, MASKED)))
            loaded.append((u, slot, blocks, m_ref[slot]))
        updated = []
        for u, slot, blocks, m in loaded:
            m_new = m
            for t, (j, row) in enumerate(blocks):
                m_new = jnp.maximum(m_new, smax_ref[u * G + t] + row)
            alpha = jnp.exp2(m - m_new)
            pv = None
            for t, (j, row) in enumerate(blocks):
                p = jnp.exp2(s_ref[u * G + t] + (row - m_new))
                part = jnp.dot(vaug_ref[j], p.astype(BF16), preferred_element_type=F32)
                pv = part if pv is None else pv + part
            acc_ref[slot] = alpha * acc_ref[slot] + pv
            updated.append((slot, m_new))
        for slot, m in updated:
            m_ref[slot] = m

    def several_steps(t, carry):
        for i in range(ATTN_BODY_STEPS):
            step = ATTN_BODY_STEPS * t + i
            cur, nxt = (even, odd) if i % 2 == 0 else (odd, even)
            scores(jnp.minimum(step + 1, n_steps - 1), *nxt)
            update(step, *cur)
        return carry

    even, odd = (s_even_ref, smax_even_ref), (s_odd_ref, smax_odd_ref)
    scores(0, *even)
    lax.fori_loop(0, n_steps // ATTN_BODY_STEPS, several_steps, 0)

    def finish_tiles(g, carry):
        for u in range(T):
            qi = g * T + u
            q0 = pl.multiple_of(qi * L, L)
            acc = acc_ref[qi]
            o_ref[0, pl.ds(q0, L), :] = (acc[0:Dh] / acc[Dh:Dh + 1]).T.astype(o_ref.dtype)
        return carry

    lax.fori_loop(0, n_blocks // T, finish_tiles, 0)


def _bf16_parts(x, n):
    parts, rest = [], np.asarray(x, np.float32)
    for _ in range(n):
        part = rest.astype(BF16).astype(np.float32)
        parts.append(part)
        rest = (rest - part).astype(np.float32)
    return np.stack(parts, axis=-1)


def _moba(k, qvt, kmean):
    B, S, _ = k.shape
    n_blocks = S // MOBA_BLOCK
    assert n_blocks % ATTN_INIT_TILES == 0 and ATTN_BODY_STEPS % 2 == 0
    assert ATTN_UNROLL * ATTN_KEY_GROUP >= ATTN_INIT_TILES
    pair_k, pair_q = _pair_schedule(n_blocks, ATTN_UNROLL, ATTN_KEY_GROUP)
    n_tiles = ATTN_UNROLL * ATTN_KEY_GROUP
    slopes = (_alibi_slopes(N_HEADS) * LOG2E).astype(np.float32)
    qext = np.zeros((N_HEADS, HEAD_DIM, MOBA_BLOCK), np.float32)
    qext[:, :ALIBI_PARTS, :] = _bf16_parts(slopes, ALIBI_PARTS)[:, :, None]
    smem = pl.BlockSpec(memory_space=pltpu.SMEM)
    return pl.pallas_call(
        _moba_kernel,
        grid=(B, N_HEADS),
        in_specs=[
            smem, smem, smem,
            pl.BlockSpec((1, HEAD_DIM, MOBA_BLOCK), lambda b, h: (h, 0, 0)),
            pl.BlockSpec((1, n_blocks, HEAD_DIM, MOBA_BLOCK), lambda b, h: (b, 0, h, 0)),
            pl.BlockSpec((1, S, HEAD_DIM), lambda b, h: (b, 0, h)),
            pl.BlockSpec((1, n_blocks, HEAD_DIM, MOBA_BLOCK), lambda b, h: (b, 0, N_HEADS + h, 0)),
            pl.BlockSpec((1, n_blocks, HEAD_DIM), lambda b, h: (b, 0, h)),
        ],
        out_specs=pl.BlockSpec((1, S, HEAD_DIM), lambda b, h: (b, 0, h)),
        out_shape=jax.ShapeDtypeStruct((B, S, ATTN_WIDTH), BF16),
        scratch_shapes=[
            pltpu.VMEM((n_blocks, 2 * HEAD_DIM, MOBA_BLOCK), BF16),
            pltpu.VMEM((S, 2 * HEAD_DIM), BF16),
            pltpu.VMEM((n_blocks, HEAD_DIM + V_PAD_ROWS, MOBA_BLOCK), BF16),
            pltpu.VMEM((n_blocks + 1, n_blocks, MOBA_BLOCK), F32),
            pltpu.VMEM((n_blocks + 2, 1, MOBA_BLOCK), F32),
            pltpu.VMEM((n_blocks + 1, HEAD_DIM + V_PAD_ROWS, MOBA_BLOCK), F32),
            pltpu.VMEM((n_tiles, MOBA_BLOCK, MOBA_BLOCK), F32),
            pltpu.VMEM((n_tiles, MOBA_BLOCK, MOBA_BLOCK), F32),
            pltpu.VMEM((n_tiles, 1, MOBA_BLOCK), F32),
            pltpu.VMEM((n_tiles, 1, MOBA_BLOCK), F32),
        ],
        compiler_params=pltpu.CompilerParams(
            dimension_semantics=("arbitrary", "arbitrary"),
            vmem_limit_bytes=V7X_VMEM_LIMIT_BYTES),
        name="moba",
    )(jnp.asarray(slopes), jnp.asarray(pair_k), jnp.asarray(pair_q), jnp.asarray(qext, BF16),
      qvt, k, qvt, kmean)


def _mix_kernel(x_ref, attn_ref, hmain_ref, hhalo_ref, wdw_ref, bdw_ref, lng_ref, lnb_ref,
                wo_ref, gpost_ref, gpre_ref, w1_ref, w2_ref,
                h1_ref, f_ref, w1b_ref, w2b_ref, ext_ref, y_ref, conv_ref, mixed_ref):
    rows = x_ref.shape[1]
    conv_ch = hmain_ref.shape[2]
    first_tile = pl.program_id(1) == 0

    w1b_ref[...] = w1_ref[...].astype(BF16)
    w2b_ref[...] = w2_ref[...].astype(BF16)

    ext_ref[0:CONV_HALO, :] = jnp.where(first_tile, 0.0, hhalo_ref[0])
    ext_ref[CONV_HALO:CONV_HALO + rows, :] = hmain_ref[0]

    attn = attn_ref[0]
    attn_w = attn.shape[1]
    d_model = wo_ref.shape[1]
    for c in range(0, d_model, MIX_COLS):
        mixed_ref[:, c:c + MIX_COLS] = jnp.dot(attn, wo_ref[0:attn_w, c:c + MIX_COLS],
                                               preferred_element_type=F32)

    shift0 = CONV_HALO - (CONV_WIDTH - 1)
    R, CL = CONV_ROW_CHUNK, CONV_LANE_CHUNK
    for r0 in range(0, rows, MIX_MATMUL_ROWS):
        for rc in range(r0, r0 + MIX_MATMUL_ROWS, R):
            for c0 in range(0, conv_ch, CL):
                y = None
                for b in range(SUBLANES):
                    z_rows = R if b == 0 else R + SUBLANES
                    z = None
                    for a in range((shift0 + CONV_WIDTH - 1) // SUBLANES + 1):
                        w = SUBLANES * a + b - shift0
                        if 0 <= w < CONV_WIDTH:
                            lo = rc + SUBLANES * a
                            tap = ext_ref[lo:lo + z_rows, c0:c0 + CL] * wdw_ref[w:w + 1, c0:c0 + CL]
                            z = tap if z is None else z + tap
                    part = z[b:b + R]
                    y = part if y is None else y + part
                y_ref[rc:rc + R, c0:c0 + CL] = y + bdw_ref[:, c0:c0 + CL]

        for rl in range(r0, r0 + MIX_MATMUL_ROWS, LN_ROW_CHUNK):
            y = y_ref[rl:rl + LN_ROW_CHUNK, :]
            mu = jnp.mean(y, axis=-1, keepdims=True)
            var = jnp.mean(jnp.square(y - mu), axis=-1, keepdims=True)
            y = (y - mu) * lax.rsqrt(var + LN_EPS) * lng_ref[...] + lnb_ref[...]
            y = y * (1.0 / (1.0 + jnp.exp(-y)))
            conv_ref[rl:rl + LN_ROW_CHUNK, :] = y.astype(BF16)

        conv = conv_ref[r0:r0 + MIX_MATMUL_ROWS, :]
        cols = []
        for c in range(0, d_model, MIX_COLS):
            r = jnp.dot(conv, wo_ref[attn_w:attn_w + conv_ch, c:c + MIX_COLS], preferred_element_type=F32)
            cols.append(mixed_ref[r0:r0 + MIX_MATMUL_ROWS, c:c + MIX_COLS] + r)
        mixed = jnp.concatenate(cols, axis=-1)
        ms = jnp.mean(mixed * mixed, axis=-1, keepdims=True)
        h1 = x_ref[0, r0:r0 + MIX_MATMUL_ROWS, :] + mixed * lax.rsqrt(ms + RMS_EPS) * gpost_ref[...]
        h1_ref[0, r0:r0 + MIX_MATMUL_ROWS, :] = h1
        ms1 = jnp.mean(h1 * h1, axis=-1, keepdims=True)
        f_ref[0, r0:r0 + MIX_MATMUL_ROWS, :] = (h1 * lax.rsqrt(ms1 + RMS_EPS) * gpre_ref[...]).astype(BF16)


def _mix(x, attn, hglu, w_dw, b_dw, ln_g, ln_b, w_out, g_post, g_pre, w_ff1, w_ff2):
    B, S, D = x.shape
    conv_ch = hglu.shape[2]
    rows = MIX_ROWS
    halo_per_tile = rows // CONV_HALO
    tiles_per_seq = S // rows
    d_ff = w_ff1.shape[1]
    ff_slice = d_ff // (B * tiles_per_seq)
    assert ff_slice * B * tiles_per_seq == d_ff and ff_slice % LANES == 0
    return pl.pallas_call(
        _mix_kernel,
        grid=(B, tiles_per_seq),
        in_specs=[
            pl.BlockSpec((1, rows, D), lambda b, i: (b, i, 0)),
            pl.BlockSpec((1, rows, attn.shape[2]), lambda b, i: (b, i, 0)),
            pl.BlockSpec((1, rows, conv_ch), lambda b, i: (b, i, 0)),
            pl.BlockSpec((1, CONV_HALO, conv_ch),
                         lambda b, i: (b, jnp.maximum(i * halo_per_tile - 1, 0), 0)),
            _resident(w_dw.shape),
            _resident((1, conv_ch)),
            _resident((1, conv_ch)),
            _resident((1, conv_ch)),
            _resident(w_out.shape),
            _resident((1, D)),
            _resident((1, D)),
            pl.BlockSpec((D, ff_slice), lambda b, i: (0, b * tiles_per_seq + i)),
            pl.BlockSpec((ff_slice, D), lambda b, i: (b * tiles_per_seq + i, 0)),
        ],
        out_specs=(
            pl.BlockSpec((1, rows, D), lambda b, i: (b, i, 0)),
            pl.BlockSpec((1, rows, D), lambda b, i: (b, i, 0)),
            pl.BlockSpec((D, ff_slice), lambda b, i: (0, b * tiles_per_seq + i)),
            pl.BlockSpec((ff_slice, D), lambda b, i: (b * tiles_per_seq + i, 0)),
        ),
        out_shape=(jax.ShapeDtypeStruct((B, S, D), F32), jax.ShapeDtypeStruct((B, S, D), BF16),
                   jax.ShapeDtypeStruct(w_ff1.shape, BF16), jax.ShapeDtypeStruct(w_ff2.shape, BF16)),
        scratch_shapes=[
            pltpu.VMEM((CONV_HALO + rows, conv_ch), F32),
            pltpu.VMEM((rows, conv_ch), F32),
            pltpu.VMEM((rows, conv_ch), BF16),
            pltpu.VMEM((rows, D), F32),
        ],
        compiler_params=pltpu.CompilerParams(
            dimension_semantics=("arbitrary", "arbitrary"),
            vmem_limit_bytes=V7X_VMEM_LIMIT_BYTES),
        name="mix",
    )(x, attn, hglu, hglu, w_dw, b_dw, ln_g, ln_b, w_out, g_post, g_pre, w_ff1, w_ff2)


def _ffn_kernel(f_ref, h1_ref, w1_ref, w2_ref, g_ref, o_ref, acc_ref):
    kf = pl.program_id(2)

    @pl.when(kf == 0)
    def _():
        acc_ref[...] = jnp.zeros_like(acc_ref)

    u = jnp.dot(f_ref[0], w1_ref[...], preferred_element_type=F32)
    u = jnp.square(jnp.maximum(u, 0.0)).astype(BF16)
    acc_ref[...] += jnp.dot(u, w2_ref[...], preferred_element_type=F32)

    @pl.when(kf == pl.num_programs(2) - 1)
    def _():
        a = acc_ref[...]
        ms = jnp.mean(a * a, axis=-1, keepdims=True)
        o_ref[0] = h1_ref[0] + a * lax.rsqrt(ms + RMS_EPS) * g_ref[...]


def _ffn(f, h1, w1, w2, g):
    B, S, D = h1.shape
    d_ff = w1.shape[1]
    rows, cols = FFN_ROWS, FFN_COLS
    return pl.pallas_call(
        _ffn_kernel,
        grid=(B, S // rows, d_ff // cols),
        in_specs=[
            pl.BlockSpec((1, rows, D), lambda b, i, k: (b, i, 0)),
            pl.BlockSpec((1, rows, D), lambda b, i, k: (b, i, 0)),
            pl.BlockSpec((D, cols), lambda b, i, k: (0, k)),
            pl.BlockSpec((cols, D), lambda b, i, k: (k, 0)),
            _resident((1, D)),
        ],
        out_specs=pl.BlockSpec((1, rows, D), lambda b, i, k: (b, i, 0)),
        out_shape=jax.ShapeDtypeStruct((B, S, D), F32),
        scratch_shapes=[pltpu.VMEM((rows, D), F32)],
        compiler_params=pltpu.CompilerParams(
            dimension_semantics=("arbitrary", "arbitrary", "arbitrary"),
            vmem_limit_bytes=V7X_VMEM_LIMIT_BYTES),
        name="ffn",
    )(f, h1, w1, w2, g)


def kernel(x, g_mix_pre, w_in, b_glu, w_dw, b_dw, ln_conv_g, ln_conv_b, w_out, g_mix_post,
           g_ffn_pre, w_ff1, w_ff2, g_ffn_post):
    B, S, D = x.shape
    depth = w_in.shape[0]
    assert S % IN_PROJ_ROWS == 0 and S % MIX_ROWS == 0 and S % FFN_ROWS == 0
    assert IN_PROJ_ROWS % MOBA_BLOCK == 0 and MIX_ROWS % CONV_HALO == 0
    assert MIX_ROWS % MIX_MATMUL_ROWS == 0 and MIX_MATMUL_ROWS % CONV_ROW_CHUNK == 0
    assert CONV_HALO >= CONV_WIDTH - 1 and CONV_HALO % SUBLANES == 0
    row = lambda v: v.reshape(1, -1)

    h = x
    for l in range(depth):
        k, qvt, kmean, hglu, w_out_b = _in_proj(h, row(g_mix_pre[l]), w_in[l].astype(BF16), row(b_glu[l]),
                                                w_out[l])
        kmean = kmean.reshape(B, S // MOBA_BLOCK, ATTN_WIDTH)
        attn = _moba(k, qvt, kmean)
        h1, f, w1b, w2b = _mix(h, attn, hglu, w_dw[l], row(b_dw[l]), row(ln_conv_g[l]), row(ln_conv_b[l]),
                               w_out_b, row(g_mix_post[l]), row(g_ffn_pre[l]), w_ff1[l], w_ff2[l])
        h = _ffn(f, h1, w1b, w2b, row(g_ffn_post[l]))
    return h
```
